```python
import math
import jax, jax.numpy as jnp
from jax import lax
import numpy as np

D_MODEL = 1024
BATCH = 8
SEQ = 8192
DEPTH = 2

SSM_EXPAND = 2
SSM_D_INNER = SSM_EXPAND * D_MODEL
SSM_HEAD_DIM = 64
SSM_HEADS = SSM_D_INNER // SSM_HEAD_DIM
SSM_GROUPS = 4
SSM_STATE = 128
SSM_CONV = 4
SSM_CHUNK = 128
SSM_CONV_DIM = SSM_D_INNER + 2 * SSM_GROUPS * SSM_STATE

DIFF_HEADS = 8
DIFF_HEAD_DIM = 64
DIFF_QK_WIDTH = DIFF_HEADS * 2 * DIFF_HEAD_DIM
DIFF_V_WIDTH = DIFF_HEADS * 2 * DIFF_HEAD_DIM
ATTN_BLOCK = 128
ROPE_THETA = 10000.0

D_FF = 4 * D_MODEL
N_ADA = 6
EPS = 1e-5

IN_PROJ_WIDTHS = (SSM_D_INNER, SSM_CONV_DIM, SSM_HEADS, DIFF_QK_WIDTH, DIFF_QK_WIDTH, DIFF_V_WIDTH, D_MODEL, D_MODEL)
IN_PROJ_WIDTH = sum(IN_PROJ_WIDTHS)

kernel_name = "hybrid_ssd_diffattn_gated_block"


def split_cols(t, widths):
    idx = []
    acc = 0
    for w in widths[:-1]:
        acc += w
        idx.append(acc)
    return jnp.split(t, idx, axis=-1)


def rms_norm(x, gain):
    xf = x.astype(jnp.float32)
    y = xf * lax.rsqrt(jnp.mean(xf * xf, axis=-1, keepdims=True) + EPS)
    return (y * gain.astype(jnp.float32)).astype(x.dtype)


def lambda_init_fn(layer_idx):
    return 0.8 - 0.6 * math.exp(-0.3 * layer_idx)


def rope_tables(seq, dim):
    inv = 1.0 / (ROPE_THETA ** (jnp.arange(0, dim, 2, dtype=jnp.float32) / dim))
    ang = jnp.arange(seq, dtype=jnp.float32)[:, None] * inv[None, :]
    return jnp.cos(ang), jnp.sin(ang)


def apply_rope(t, cos, sin):
    half = t.shape[-1] // 2
    cos = cos[None, :, None, None, :]
    sin = sin[None, :, None, None, :]
    tf = t.astype(jnp.float32)
    t1, t2 = tf[..., :half], tf[..., half:]
    return jnp.concatenate([t1 * cos - t2 * sin, t2 * cos + t1 * sin], axis=-1).astype(t.dtype)


def causal_dwconv(x, w, b):
    k = w.shape[0]
    y = lax.conv_general_dilated(
        x, w[:, None, :].astype(x.dtype), window_strides=(1,), padding=[(k - 1, 0)],
        dimension_numbers=("NWC", "WIO", "NWC"), feature_group_count=x.shape[-1])
    return y + b.astype(x.dtype)


def ssd_chunked(xh, dt, a, bm, cm):
    bsz, s, h, p = xh.shape
    g, n = bm.shape[-2], bm.shape[-1]
    r = h // g
    nc, l = s // SSM_CHUNK, SSM_CHUNK
    X = (xh * dt[..., None]).reshape(bsz, nc, l, g, r, p)
    a_step = (dt * a).reshape(bsz, nc, l, g, r)
    Bc = bm.reshape(bsz, nc, l, g, n)
    Cc = cm.reshape(bsz, nc, l, g, n)
    a_cum = jnp.cumsum(a_step, axis=2)
    causal = jnp.tril(jnp.ones((l, l), dtype=bool))[None, None, :, :, None, None]
    seg = a_cum[:, :, :, None] - a_cum[:, :, None, :]
    decay = jnp.exp(jnp.where(causal, seg, -jnp.inf))
    cb = jnp.einsum("bclgn,bcsgn->bclsg", Cc, Bc)
    y_diag = jnp.einsum("bclsgr,bcsgrp->bclgrp", cb[..., None] * decay, X)
    decay_to_end = jnp.exp(a_cum[:, :, -1:] - a_cum)
    states = jnp.einsum("bclgn,bclgrp->bcgrpn", Bc, X * decay_to_end[..., None])
    chunk_decay = jnp.exp(a_cum[:, :, -1])

    def step(hstate, inp):
        s_c, d_c = inp
        return hstate * d_c[..., None, None] + s_c, hstate

    h0 = jnp.zeros((bsz, g, r, p, n), dtype=X.dtype)
    _, prev = lax.scan(step, h0, (jnp.moveaxis(states, 1, 0), jnp.moveaxis(chunk_decay, 1, 0)))
    prev = jnp.moveaxis(prev, 0, 1)
    y_off = jnp.einsum("bclgn,bcgrpn->bclgrp", Cc, prev) * jnp.exp(a_cum)[..., None]
    return (y_diag + y_off).reshape(bsz, s, h, p)


def mamba2_branch(z, xbc, dt_raw, conv_w, conv_b, dt_bias, a_log, d_skip, norm_g):
    bsz, s, _ = z.shape
    xbc = jax.nn.silu(causal_dwconv(xbc, conv_w, conv_b))
    xs, bm, cm = split_cols(xbc, (SSM_D_INNER, SSM_GROUPS * SSM_STATE, SSM_GROUPS * SSM_STATE))
    xh = xs.reshape(bsz, s, SSM_HEADS, SSM_HEAD_DIM).astype(jnp.float32)
    bm = bm.reshape(bsz, s, SSM_GROUPS, SSM_STATE).astype(jnp.float32)
    cm = cm.reshape(bsz, s, SSM_GROUPS, SSM_STATE).astype(jnp.float32)
    dt = jax.nn.softplus(dt_raw.astype(jnp.float32) + dt_bias.astype(jnp.float32))
    a = -jnp.exp(a_log.astype(jnp.float32))
    y = ssd_chunked(xh, dt, a, bm, cm) + d_skip.astype(jnp.float32)[:, None] * xh
    y = y.reshape(bsz, s, SSM_D_INNER) * jax.nn.silu(z.astype(jnp.float32))
    yg = y.reshape(bsz, s, SSM_GROUPS, SSM_D_INNER // SSM_GROUPS)
    yg = yg * lax.rsqrt(jnp.mean(yg * yg, axis=-1, keepdims=True) + EPS)
    y = yg.reshape(bsz, s, SSM_D_INNER) * norm_g.astype(jnp.float32)
    return y.astype(z.dtype)


def diff_attention_branch(q, k, v, cos, sin, lam_q1, lam_k1, lam_q2, lam_k2, subln_g, lambda_init):
    bsz, s, _ = q.shape
    h, d = DIFF_HEADS, DIFF_HEAD_DIM
    nb = s // ATTN_BLOCK
    q = apply_rope(q.reshape(bsz, s, h, 2, d), cos, sin)
    k = apply_rope(k.reshape(bsz, s, h, 2, d), cos, sin)
    v = v.reshape(bsz, s, h, 2 * d)
    lam = (jnp.exp(jnp.sum(lam_q1.astype(jnp.float32) * lam_k1.astype(jnp.float32)))
           - jnp.exp(jnp.sum(lam_q2.astype(jnp.float32) * lam_k2.astype(jnp.float32)))
           + lambda_init)
    qb = q.reshape(bsz, nb, ATTN_BLOCK, h, 2, d).transpose(1, 0, 3, 4, 2, 5)
    kt = k.transpose(0, 2, 3, 1, 4)
    vt = v.transpose(0, 2, 1, 3)
    key_pos = jnp.arange(s)
    scale = d ** -0.5

    def block(args):
        q_blk, start = args
        sc = jnp.einsum("bhiqd,bhikd->bhiqk", q_blk, kt).astype(jnp.float32) * scale
        q_pos = start + jnp.arange(ATTN_BLOCK)
        mask = key_pos[None, :] <= q_pos[:, None]
        pr = jax.nn.softmax(jnp.where(mask, sc, -jnp.inf), axis=-1)
        attn = pr[:, :, 0] - lam * pr[:, :, 1]
        return jnp.einsum("bhqk,bhkv->bhqv", attn.astype(vt.dtype), vt)

    starts = jnp.arange(nb) * ATTN_BLOCK
    o = lax.map(block, (qb, starts))
    o = o.transpose(1, 0, 3, 2, 4).reshape(bsz, s, h, 2 * d)
    o = rms_norm(o, subln_g) * (1.0 - lambda_init)
    return o.reshape(bsz, s, h * 2 * d)


def setup_inputs(seed: int = 0) -> dict:
    key = jax.random.key(seed)
    ks = jax.random.split(key, 32)
    f32 = jnp.float32
    L = DEPTH

    def nrm(k, shape, scale):
        return jax.random.normal(k, shape, f32) * scale

    u_dt = jax.random.uniform(ks[10], (L, SSM_HEADS), f32)
    dt0 = jnp.exp(u_dt * (math.log(0.1) - math.log(0.001)) + math.log(0.001))
    dt_bias = dt0 + jnp.log(-jnp.expm1(-dt0))
    a0 = jax.random.uniform(ks[11], (L, SSM_HEADS), f32, 1.0, 16.0)
    return {
        "x": nrm(ks[0], (BATCH, SEQ, D_MODEL), 1.0),
        "c": nrm(ks[1], (BATCH, D_MODEL), 1.0),
        "w_ada": nrm(ks[2], (L, D_MODEL, N_ADA * D_MODEL), 0.5 * D_MODEL ** -0.5),
        "b_ada": nrm(ks[3], (L, N_ADA * D_MODEL), 0.02),
        "norm1_g": 1.0 + nrm(ks[4], (L, D_MODEL), 0.05),
        "w_in": nrm(ks[5], (L, D_MODEL, IN_PROJ_WIDTH), D_MODEL ** -0.5),
        "conv_w": nrm(ks[6], (L, SSM_CONV, SSM_CONV_DIM), SSM_CONV ** -0.5),
        "conv_b": nrm(ks[7], (L, SSM_CONV_DIM), 0.02),
        "dt_bias": dt_bias,
        "a_log": jnp.log(a0),
        "d_skip": 1.0 + nrm(ks[12], (L, SSM_HEADS), 0.1),
        "ssm_norm_g": 1.0 + nrm(ks[13], (L, SSM_D_INNER), 0.05),
        "lam_q1": nrm(ks[14], (L, DIFF_HEAD_DIM), 0.1),
        "lam_k1": nrm(ks[15], (L, DIFF_HEAD_DIM), 0.1),
        "lam_q2": nrm(ks[16], (L, DIFF_HEAD_DIM), 0.1),
        "lam_k2": nrm(ks[17], (L, DIFF_HEAD_DIM), 0.1),
        "subln_g": 1.0 + nrm(ks[18], (L, 2 * DIFF_HEAD_DIM), 0.05),
        "w_ssm_out": nrm(ks[19], (L, SSM_D_INNER, D_MODEL), SSM_D_INNER ** -0.5),
        "w_attn_out": nrm(ks[20], (L, DIFF_V_WIDTH, D_MODEL), DIFF_V_WIDTH ** -0.5),
        "w_out": nrm(ks[21], (L, D_MODEL, D_MODEL), D_MODEL ** -0.5),
        "norm2_g": 1.0 + nrm(ks[22], (L, D_MODEL), 0.05),
        "w_ff1": nrm(ks[23], (L, D_MODEL, D_FF), D_MODEL ** -0.5),
        "w_ff2": nrm(ks[24], (L, D_FF, D_MODEL), D_FF ** -0.5),
        "final_g": 1.0 + nrm(ks[25], (D_MODEL,), 0.05),
    }


def reference(x, c, w_ada, b_ada, norm1_g, w_in, conv_w, conv_b, dt_bias, a_log, d_skip,
              ssm_norm_g, lam_q1, lam_k1, lam_q2, lam_k2, subln_g, w_ssm_out, w_attn_out,
              w_out, norm2_g, w_ff1, w_ff2, final_g):
    s = x.shape[1]
    cos, sin = rope_tables(s, DIFF_HEAD_DIM)
    c_act = jax.nn.silu(c)
    for l in range(DEPTH):
        ada = c_act @ w_ada[l] + b_ada[l]
        sh1, sc1, g1, sh2, sc2, g2 = jnp.split(ada, N_ADA, axis=-1)
        h = rms_norm(x, norm1_g[l]) * (1.0 + sc1[:, None]) + sh1[:, None]
        proj = h @ w_in[l]
        z, xbc, dt_raw, q, k, v, gate_s, gate_a = split_cols(proj, IN_PROJ_WIDTHS)
        y_ssm = mamba2_branch(z, xbc, dt_raw, conv_w[l], conv_b[l], dt_bias[l], a_log[l],
                              d_skip[l], ssm_norm_g[l]) @ w_ssm_out[l]
        y_att = diff_attention_branch(q, k, v, cos, sin, lam_q1[l], lam_k1[l], lam_q2[l],
                                      lam_k2[l], subln_g[l], lambda_init_fn(l)) @ w_attn_out[l]
        mixed = jax.nn.sigmoid(gate_s) * y_ssm + jax.nn.sigmoid(gate_a) * y_att
        x = x + g1[:, None] * (mixed @ w_out[l])
        h = rms_norm(x, norm2_g[l]) * (1.0 + sc2[:, None]) + sh2[:, None]
        u = jnp.square(jax.nn.relu(h @ w_ff1[l]))
        x = x + g2[:, None] * (u @ w_ff2[l])
    return rms_norm(x, final_g)
```

```python
import functools
import math

import jax
import jax.numpy as jnp
from jax import lax
from jax.experimental import pallas as pl
from jax.experimental.pallas import tpu as pltpu

F32 = jnp.float32
BF16 = jnp.bfloat16
HIGHEST = lax.Precision.HIGHEST

D_MODEL = 1024
SSM_D_INNER = 2048
SSM_HEAD_DIM = 64
SSM_HEADS = 32
SSM_GROUPS = 4
SSM_STATE = 128
SSM_CONV = 4
SSM_CHUNK = 128
DIFF_HEADS = 8
DIFF_HEAD_DIM = 64
D_FF = 4096
N_ADA = 6
EPS = 1e-5
ROPE_THETA = 10000.0
DT_OFF = 5120
DT_W = SSM_HEADS

LANES = 128
HALO = 8
VMEM_LIMIT = 56 * 1024 * 1024

COL_Z, COL_XS, COL_BC, COL_Q, COL_K, COL_V, COL_GS, COL_GA = (
    0, 2048, 4096, 5120, 6144, 7168, 8192, 9216)
PROJ_W = 10240

LOG2E = 1.4426950408889634


def _cparams(sem):
    return pltpu.CompilerParams(dimension_semantics=sem, vmem_limit_bytes=VMEM_LIMIT)


def _ada_kernel(c_ref, w_ref, b_ref, o_ref):
    c = c_ref[...]
    ca = c * jax.nn.sigmoid(c)
    o_ref[0] = jnp.dot(ca, w_ref[0], precision=HIGHEST,
                       preferred_element_type=F32) + b_ref[0]


def _ada(c, w_ada, b_ada):
    depth, d, n = w_ada.shape
    bsz = c.shape[0]
    tn = 1024
    return pl.pallas_call(
        _ada_kernel,
        grid=(depth, n // tn),
        in_specs=[
            pl.BlockSpec((bsz, d), lambda l, j: (0, 0)),
            pl.BlockSpec((1, d, tn), lambda l, j: (l, 0, j)),
            pl.BlockSpec((1, 1, tn), lambda l, j: (l, 0, j)),
        ],
        out_specs=pl.BlockSpec((1, bsz, tn), lambda l, j: (l, 0, j)),
        out_shape=jax.ShapeDtypeStruct((depth, bsz, n), F32),
        compiler_params=_cparams(("arbitrary", "arbitrary")),
        name="ada",
    )(c, w_ada, b_ada.reshape(depth, 1, n))


def _rope_tile(acc, cos, sin, scale):
    lane = lax.broadcasted_iota(jnp.int32, cos.shape, 1)
    first_half = (lane % DIFF_HEAD_DIM) < (DIFF_HEAD_DIM // 2)
    outs = []
    for hh in range(DIFF_HEADS):
        t = acc[:, hh * LANES:(hh + 1) * LANES]
        sw = jnp.where(first_half, pltpu.roll(t, LANES - 32, 1), pltpu.roll(t, 32, 1))
        outs.append((t * cos + sw * sin) * scale)
    return jnp.concatenate(outs, axis=1)


def _inproj_kernel(x_ref, g_ref, sc_ref, sh_ref, w_ref, wdt_ref, cos_ref, sin_ref,
                   o_ref, dt_ref, h_scr, *, jq, jk, qscale):
    j = pl.program_id(1)

    @pl.when(j == 0)
    def _():
        xf = x_ref[...]
        ms = jnp.mean(xf * xf, axis=-1, keepdims=True)
        h = xf * lax.rsqrt(ms + EPS) * g_ref[...]
        h = h * (1.0 + sc_ref[0]) + sh_ref[0]
        hb = h.astype(BF16)
        h_scr[...] = hb
        dt_ref[...] = jnp.dot(hb, wdt_ref[...], preferred_element_type=F32)

    acc = jnp.dot(h_scr[...], w_ref[...], preferred_element_type=F32)
    is_rope = jnp.logical_or(j == jq, j == jk)

    @pl.when(jnp.logical_not(is_rope))
    def _():
        o_ref[...] = acc.astype(o_ref.dtype)

    @pl.when(is_rope)
    def _():
        scale = jnp.where(j == jq, qscale, 1.0).astype(F32)
        o_ref[...] = _rope_tile(acc, cos_ref[...], sin_ref[...], scale).astype(o_ref.dtype)


def _inproj(x2, g, sc, sh, w_main, w_dt, cos_t, sin_t, seq):
    t, d = x2.shape
    tm, tn = 1024, 1024
    nps = seq // tm
    qscale = (DIFF_HEAD_DIM ** -0.5) * LOG2E
    kern = functools.partial(_inproj_kernel, jq=COL_Q // tn, jk=COL_K // tn, qscale=qscale)
    return pl.pallas_call(
        kern,
        grid=(t // tm, PROJ_W // tn),
        in_specs=[
            pl.BlockSpec((tm, d), lambda i, j: (i, 0)),
            pl.BlockSpec((1, d), lambda i, j: (0, 0)),
            pl.BlockSpec((1, 1, d), lambda i, j: (i // nps, 0, 0)),
            pl.BlockSpec((1, 1, d), lambda i, j: (i // nps, 0, 0)),
            pl.BlockSpec((d, tn), lambda i, j: (0, j)),
            pl.BlockSpec((d, LANES), lambda i, j: (0, 0)),
            pl.BlockSpec((tm, LANES), lambda i, j: (i % nps, 0)),
            pl.BlockSpec((tm, LANES), lambda i, j: (i % nps, 0)),
        ],
        out_specs=[
            pl.BlockSpec((tm, tn), lambda i, j: (i, j)),
            pl.BlockSpec((tm, LANES), lambda i, j: (i, 0)),
        ],
        out_shape=[
            jax.ShapeDtypeStruct((t, PROJ_W), BF16),
            jax.ShapeDtypeStruct((t, LANES), F32),
        ],
        scratch_shapes=[pltpu.VMEM((tm, d), BF16)],
        compiler_params=_cparams(("arbitrary", "arbitrary")),
        name="inproj",
    )(x2, g, sc, sh, w_main, w_dt, cos_t, sin_t)


def _conv_silu(x_ref, pad_scr, w_ref, b_ref):
    lc = x_ref.shape[0]
    pad_scr[HALO:HALO + lc, :] = x_ref[...].astype(F32)
    acc = b_ref[...] + w_ref[SSM_CONV - 1:SSM_CONV, :] * pad_scr[HALO:HALO + lc, :]
    for k in range(SSM_CONV - 1):
        off = HALO - (SSM_CONV - 1) + k
        acc = acc + w_ref[k:k + 1, :] * pad_scr[off:off + lc, :]
    pad_scr[HALO - (SSM_CONV - 1):HALO, :] = pad_scr[HALO + lc - (SSM_CONV - 1):HALO + lc, :]
    return acc * jax.nn.sigmoid(acc)


def _softplus(x):
    return jnp.maximum(x, 0.0) + jnp.log1p(jnp.exp(-jnp.abs(x)))


def _ssd_kernel(z_ref, xs_ref, bc_ref, dt_ref, cwx_ref, cbx_ref, cwb_ref, cbb_ref,
                dtb_ref, alog_ref, dexp_ref, ng_ref, o_ref,
                padx_scr, padb_scr, state_scr, y_scr):
    lc = SSM_CHUNK
    n = SSM_STATE

    @pl.when(pl.program_id(1) == 0)
    def _():
        state_scr[...] = jnp.zeros_like(state_scr)
        padx_scr[0:HALO, :] = jnp.zeros((HALO, padx_scr.shape[1]), F32)
        padb_scr[0:HALO, :] = jnp.zeros((HALO, padb_scr.shape[1]), F32)

    xs = _conv_silu(xs_ref, padx_scr, cwx_ref, cbx_ref)
    bc = _conv_silu(bc_ref, padb_scr, cwb_ref, cbb_ref)
    xs_b = xs.astype(BF16)

    dtp = _softplus(dt_ref[...] + dtb_ref[...])
    a_step = dtp * (-jnp.exp(alog_ref[...]))
    row_i = lax.broadcasted_iota(jnp.int32, (lc, lc), 0)
    col_i = lax.broadcasted_iota(jnp.int32, (lc, lc), 1)
    tril = row_i >= col_i
    tril_f = tril.astype(F32)
    triu_f = (row_i <= col_i).astype(F32)
    acum = jnp.dot(tril_f, a_step, precision=HIGHEST, preferred_element_type=F32)
    nh8 = SSM_HEADS
    a_step_t = a_step.T[:nh8, :]
    dt_t = dtp.T[:nh8, :]
    acum_t = jnp.dot(a_step_t, triu_f, precision=HIGHEST, preferred_element_type=F32)
    alast_t = acum_t[:, lc - 1:lc]
    w2_t = dt_t * jnp.exp(alast_t - acum_t)
    e1 = jnp.exp(acum)
    dlast = jnp.broadcast_to(jnp.exp(alast_t), (nh8, LANES))

    lane = lax.broadcasted_iota(jnp.int32, (1, LANES), 1)
    lo = lane < SSM_HEAD_DIM
    heads_per_group = SSM_HEADS // SSM_GROUPS
    for g in range(SSM_GROUPS):
        bm = bc[:, g * n:(g + 1) * n]
        cm = bc[:, SSM_GROUPS * n + g * n:SSM_GROUPS * n + (g + 1) * n]
        bm_t = bm.T
        cb = jnp.dot(cm.astype(BF16), bm_t.astype(BF16), preferred_element_type=F32)
        for pp in range(heads_per_group // 2):
            p = g * (heads_per_group // 2) + pp
            cs = slice(p * LANES, (p + 1) * LANES)
            x_pair = xs_b[:, cs]
            s_pair = state_scr[:, cs]
            s_pair_b = s_pair.astype(BF16)
            y = jnp.zeros((lc, LANES), F32)
            s_new = jnp.zeros((n, LANES), F32)
            for e in range(2):
                h = 2 * p + e
                sel = lo if e == 0 else jnp.logical_not(lo)
                x_h = jnp.where(sel, x_pair, jnp.zeros_like(x_pair))
                s_h = jnp.where(sel, s_pair_b, jnp.zeros_like(s_pair_b))
                seg = acum[:, h:h + 1] - acum_t[h:h + 1, :]
                decay = jnp.exp(jnp.where(tril, seg, -1e30))
                m_h = (decay * (cb * dt_t[h:h + 1, :])).astype(BF16)
                y = y + jnp.dot(m_h, x_h, preferred_element_type=F32)
                c_h = (cm * e1[:, h:h + 1]).astype(BF16)
                y = y + jnp.dot(c_h, s_h, preferred_element_type=F32)
                b_h = (bm_t * w2_t[h:h + 1, :]).astype(BF16)
                s_new = s_new + jnp.dot(b_h, x_h, preferred_element_type=F32)
            d_row = jnp.where(lo, dlast[2 * p:2 * p + 1, :], dlast[2 * p + 1:2 * p + 2, :])
            state_scr[:, cs] = s_pair * d_row + s_new
            y_scr[:, cs] = y + dexp_ref[:, cs] * xs[:, cs]

    zf = z_ref[...].astype(F32)
    yg = y_scr[...] * (zf * jax.nn.sigmoid(zf))
    gw = SSM_D_INNER // SSM_GROUPS
    outs = []
    for g in range(SSM_GROUPS):
        blk = yg[:, g * gw:(g + 1) * gw]
        ms = jnp.mean(blk * blk, axis=-1, keepdims=True)
        outs.append(blk * lax.rsqrt(ms + EPS))
    o_ref[...] = (jnp.concatenate(outs, axis=1) * ng_ref[...]).astype(o_ref.dtype)


def _ssd(proj, dt, cwx, cbx, cwb, cbb, dtb, alog, dexp, ng, bsz, seq):
    t = proj.shape[0]
    lc = SSM_CHUNK
    nc = seq // lc
    xw, bw = SSM_D_INNER, 2 * SSM_GROUPS * SSM_STATE
    row = lambda b, c: b * nc + c
    full = lambda b, c: (0, 0)
    return pl.pallas_call(
        _ssd_kernel,
        grid=(bsz, nc),
        in_specs=[
            pl.BlockSpec((lc, xw), lambda b, c: (row(b, c), COL_Z // xw)),
            pl.BlockSpec((lc, xw), lambda b, c: (row(b, c), COL_XS // xw)),
            pl.BlockSpec((lc, bw), lambda b, c: (row(b, c), COL_BC // bw)),
            pl.BlockSpec((lc, LANES), lambda b, c: (row(b, c), 0)),
            pl.BlockSpec((SSM_CONV, xw), full),
            pl.BlockSpec((1, xw), full),
            pl.BlockSpec((SSM_CONV, bw), full),
            pl.BlockSpec((1, bw), full),
            pl.BlockSpec((1, LANES), full),
            pl.BlockSpec((1, LANES), full),
            pl.BlockSpec((1, xw), full),
            pl.BlockSpec((1, xw), full),
        ],
        out_specs=pl.BlockSpec((lc, xw), lambda b, c: (row(b, c), 0)),
        out_shape=jax.ShapeDtypeStruct((t, xw), BF16),
        scratch_shapes=[
            pltpu.VMEM((HALO + lc, xw), F32),
            pltpu.VMEM((HALO + lc, bw), F32),
            pltpu.VMEM((SSM_STATE, xw), F32),
            pltpu.VMEM((lc, xw), F32),
        ],
        compiler_params=_cparams(("arbitrary", "arbitrary")),
        name="ssd",
    )(proj, proj, proj, dt, cwx, cbx, cwb, cbb, dtb, alog, dexp, ng)


def _attn_kernel(q_ref, k_ref, v_ref, lam_ref, sg_ref, o_ref, m_scr, l_scr, acc_scr,
                 *, tq, tk, lam_init):
    qi = pl.program_id(2)
    q = q_ref[...]
    lane = lax.broadcasted_iota(jnp.int32, q.shape, 1)
    zero = jnp.zeros_like(q)
    qq = jnp.concatenate([jnp.where(lane < DIFF_HEAD_DIM, q, zero),
                          jnp.where(lane >= DIFF_HEAD_DIM, q, zero)], axis=0)

    m_scr[...] = jnp.full(m_scr.shape, -1e30, F32)
    l_scr[...] = jnp.zeros(l_scr.shape, F32)
    acc_scr[...] = jnp.zeros(acc_scr.shape, F32)

    def step(j, masked):
        start = pl.multiple_of(j * tk, tk)
        kb = k_ref[pl.ds(start, tk), :]
        vb = v_ref[pl.ds(start, tk), :]
        s = lax.dot_general(qq, kb, (((1,), (1,)), ((), ())), preferred_element_type=F32)
        if masked:
            r = lax.broadcasted_iota(jnp.int32, (tq, tk), 0)
            c = lax.broadcasted_iota(jnp.int32, (tq, tk), 1)
            ok = c <= r
            ok2 = jnp.concatenate([ok, ok], axis=0)
            s = jnp.where(ok2, s, -1e30)
        m_old = m_scr[...]
        m_new = jnp.maximum(m_old, jnp.max(s, axis=-1, keepdims=True))
        alpha = jnp.exp2(m_old - m_new)
        p = jnp.exp2(s - m_new)
        l_scr[...] = alpha * l_scr[...] + jnp.sum(p, axis=-1, keepdims=True)
        acc_scr[...] = alpha * acc_scr[...] + jnp.dot(p.astype(BF16), vb,
                                                      preferred_element_type=F32)
        m_scr[...] = m_new

    def body(j, carry):
        step(j, False)
        return carry

    lax.fori_loop(0, qi, body, 0)
    step(qi, True)

    lam = (jnp.exp(jnp.sum(lam_ref[0:1, :] * lam_ref[1:2, :], axis=-1, keepdims=True))
           - jnp.exp(jnp.sum(lam_ref[2:3, :] * lam_ref[3:4, :], axis=-1, keepdims=True))
           + lam_init)
    o_all = acc_scr[...] / l_scr[...]
    o = o_all[:tq] - lam * o_all[tq:]
    ms = jnp.mean(o * o, axis=-1, keepdims=True)
    o = o * lax.rsqrt(ms + EPS) * sg_ref[...] * (1.0 - lam_init)
    o_ref[...] = o.astype(o_ref.dtype)


def _attn(proj, lam4, sg, bsz, seq, lam_init):
    t = proj.shape[0]
    tq = tk = 512
    nq = seq // tq
    kern = functools.partial(_attn_kernel, tq=tq, tk=tk, lam_init=lam_init)
    return pl.pallas_call(
        kern,
        grid=(bsz, DIFF_HEADS, nq),
        in_specs=[
            pl.BlockSpec((tq, LANES), lambda b, h, i: (b * nq + i, COL_Q // LANES + h)),
            pl.BlockSpec((seq, LANES), lambda b, h, i: (b, COL_K // LANES + h)),
            pl.BlockSpec((seq, LANES), lambda b, h, i: (b, COL_V // LANES + h)),
            pl.BlockSpec((4, DIFF_HEAD_DIM), lambda b, h, i: (0, 0)),
            pl.BlockSpec((1, LANES), lambda b, h, i: (0, 0)),
        ],
        out_specs=pl.BlockSpec((tq, LANES), lambda b, h, i: (b * nq + i, h)),
        out_shape=jax.ShapeDtypeStruct((t, DIFF_HEADS * LANES), BF16),
        scratch_shapes=[
            pltpu.VMEM((2 * tq, 1), F32),
            pltpu.VMEM((2 * tq, 1), F32),
            pltpu.VMEM((2 * tq, LANES), F32),
        ],
        compiler_params=_cparams(("arbitrary", "arbitrary", "arbitrary")),
        name="attn",
    )(proj, proj, proj, lam4, sg)


def _merge_kernel(x_ref, ys_ref, ya_ref, gs_ref, ga_ref, g1_ref, ws_ref, wa_ref, wo_ref, o_ref):
    ys = jnp.dot(ys_ref[...], ws_ref[...], preferred_element_type=F32)
    ya = jnp.dot(ya_ref[...], wa_ref[...], preferred_element_type=F32)
    mixed = (jax.nn.sigmoid(gs_ref[...].astype(F32)) * ys
             + jax.nn.sigmoid(ga_ref[...].astype(F32)) * ya)
    upd = jnp.dot(mixed.astype(BF16), wo_ref[...], preferred_element_type=F32)
    o_ref[...] = x_ref[...] + g1_ref[0] * upd


def _merge(x2, yssm, yatt, proj, g1, ws, wa, wo, seq):
    t, d = x2.shape
    tm = 512
    nps = seq // tm
    const = lambda i: (0, 0)
    return pl.pallas_call(
        _merge_kernel,
        grid=(t // tm,),
        in_specs=[
            pl.BlockSpec((tm, d), lambda i: (i, 0)),
            pl.BlockSpec((tm, SSM_D_INNER), lambda i: (i, 0)),
            pl.BlockSpec((tm, d), lambda i: (i, 0)),
            pl.BlockSpec((tm, d), lambda i: (i, COL_GS // d)),
            pl.BlockSpec((tm, d), lambda i: (i, COL_GA // d)),
            pl.BlockSpec((1, 1, d), lambda i: (i // nps, 0, 0)),
            pl.BlockSpec((SSM_D_INNER, d), const),
            pl.BlockSpec((d, d), const),
            pl.BlockSpec((d, d), const),
        ],
        out_specs=pl.BlockSpec((tm, d), lambda i: (i, 0)),
        out_shape=jax.ShapeDtypeStruct((t, d), F32),
        compiler_params=_cparams(("arbitrary",)),
        name="merge",
    )(x2, yssm, yatt, proj, proj, g1, ws, wa, wo)


def _mlp_kernel(x_ref, g_ref, sc_ref, sh_ref, g2_ref, w1_ref, w2_ref, fg_ref, o_ref, *, final):
    xf = x_ref[...]
    ms = jnp.mean(xf * xf, axis=-1, keepdims=True)
    h = xf * lax.rsqrt(ms + EPS) * g_ref[...]
    h = h * (1.0 + sc_ref[0]) + sh_ref[0]
    u = jnp.dot(h.astype(BF16), w1_ref[...], preferred_element_type=F32)
    u = jnp.square(jnp.maximum(u, 0.0))
    y = xf + g2_ref[0] * jnp.dot(u.astype(BF16), w2_ref[...], preferred_element_type=F32)
    if final:
        ms2 = jnp.mean(y * y, axis=-1, keepdims=True)
        y = y * lax.rsqrt(ms2 + EPS) * fg_ref[...]
    o_ref[...] = y


def _mlp(x2, g, sc, sh, g2, w1, w2, fg, seq, final):
    t, d = x2.shape
    tm = 512
    nps = seq // tm
    const = lambda i: (0, 0)
    per_b = lambda i: (i // nps, 0, 0)
    return pl.pallas_call(
        functools.partial(_mlp_kernel, final=final),
        grid=(t // tm,),
        in_specs=[
            pl.BlockSpec((tm, d), lambda i: (i, 0)),
            pl.BlockSpec((1, d), const),
            pl.BlockSpec((1, 1, d), per_b),
            pl.BlockSpec((1, 1, d), per_b),
            pl.BlockSpec((1, 1, d), per_b),
            pl.BlockSpec((d, D_FF), const),
            pl.BlockSpec((D_FF, d), const),
            pl.BlockSpec((1, d), const),
        ],
        out_specs=pl.BlockSpec((tm, d), lambda i: (i, 0)),
        out_shape=jax.ShapeDtypeStruct((t, d), F32),
        compiler_params=_cparams(("arbitrary",)),
        name="mlp",
    )(x2, g, sc, sh, g2, w1, w2, fg)


def _rope_tables(seq):
    half = DIFF_HEAD_DIM // 2
    inv = 1.0 / (ROPE_THETA ** (jnp.arange(0, DIFF_HEAD_DIM, 2, dtype=F32) / DIFF_HEAD_DIM))
    ang = jnp.arange(seq, dtype=F32)[:, None] * inv[None, :]
    cos, sin = jnp.cos(ang), jnp.sin(ang)
    reps = LANES // half
    cos_t = jnp.tile(cos, (1, reps))
    sin_t = jnp.tile(jnp.concatenate([-sin, sin], axis=1), (1, reps // 2))
    return cos_t, sin_t


def _lambda_init(layer_idx):
    return 0.8 - 0.6 * math.exp(-0.3 * layer_idx)


def kernel(x, c, w_ada, b_ada, norm1_g, w_in, conv_w, conv_b, dt_bias, a_log, d_skip,
           ssm_norm_g, lam_q1, lam_k1, lam_q2, lam_k2, subln_g, w_ssm_out, w_attn_out,
           w_out, norm2_g, w_ff1, w_ff2, final_g):
    bsz, seq, d = x.shape
    depth = w_in.shape[0]
    assert d == D_MODEL and seq % 1024 == 0
    t = bsz * seq
    cos_t, sin_t = _rope_tables(seq)
    ada = _ada(c, w_ada, b_ada)
    x2 = x.reshape(t, d)
    pad_h = LANES - SSM_HEADS
    for l in range(depth):
        mods = [ada[l, :, i * d:(i + 1) * d].reshape(bsz, 1, d) for i in range(N_ADA)]
        sh1, sc1, g1, sh2, sc2, g2 = mods
        w_l = w_in[l]
        w_main = jnp.concatenate([w_l[:, :DT_OFF], w_l[:, DT_OFF + DT_W:]], axis=1).astype(BF16)
        w_dt = jnp.pad(w_l[:, DT_OFF:DT_OFF + DT_W], ((0, 0), (0, pad_h))).astype(BF16)
        proj, dt = _inproj(x2, norm1_g[l].reshape(1, d), sc1, sh1, w_main, w_dt,
                           cos_t, sin_t, seq)
        cw, cb = conv_w[l], conv_b[l].reshape(1, -1)
        yssm = _ssd(
            proj, dt,
            cw[:, :SSM_D_INNER], cb[:, :SSM_D_INNER], cw[:, SSM_D_INNER:], cb[:, SSM_D_INNER:],
            jnp.pad(dt_bias[l], (0, pad_h)).reshape(1, LANES),
            jnp.pad(a_log[l], (0, pad_h)).reshape(1, LANES),
            jnp.repeat(d_skip[l], SSM_HEAD_DIM).reshape(1, SSM_D_INNER),
            ssm_norm_g[l].reshape(1, SSM_D_INNER), bsz, seq)
        lam4 = jnp.stack([lam_q1[l], lam_k1[l], lam_q2[l], lam_k2[l]], axis=0)
        yatt = _attn(proj, lam4, subln_g[l].reshape(1, LANES), bsz, seq, _lambda_init(l))
        x2 = _merge(x2, yssm, yatt, proj, g1, w_ssm_out[l].astype(BF16),
                    w_attn_out[l].astype(BF16), w_out[l].astype(BF16), seq)
        x2 = _mlp(x2, norm2_g[l].reshape(1, d), sc2, sh2, g2, w_ff1[l].astype(BF16),
                  w_ff2[l].astype(BF16), final_g.reshape(1, d), seq, l == depth - 1)
    return x2.reshape(bsz, seq, d)
```

```python
import functools
import math

import jax
import jax.numpy as jnp
from jax import lax
from jax.experimental import pallas as pl
from jax.experimental.pallas import tpu as pltpu

F32 = jnp.float32
BF16 = jnp.bfloat16
HIGHEST = lax.Precision.HIGHEST

D_MODEL = 1024
SSM_D_INNER = 2048
SSM_HEAD_DIM = 64
SSM_HEADS = 32
SSM_GROUPS = 4
SSM_STATE = 128
SSM_CONV = 4
SSM_CHUNK = 128
DIFF_HEADS = 8
DIFF_HEAD_DIM = 64
D_FF = 4096
N_ADA = 6
EPS = 1e-5
ROPE_THETA = 10000.0
DT_OFF = 5120
DT_W = SSM_HEADS

LANES = 128
HALO = 8
VMEM_LIMIT = 56 * 1024 * 1024

COL_Z, COL_XS, COL_BC, COL_Q, COL_K, COL_V, COL_GS, COL_GA = (
    0, 2048, 4096, 5120, 6144, 7168, 8192, 9216)
PROJ_W = 10240

LOG2E = 1.4426950408889634


def _cparams(sem):
    return pltpu.CompilerParams(dimension_semantics=sem, vmem_limit_bytes=VMEM_LIMIT)


def _ada_kernel(c_ref, w_ref, b_ref, o_ref):
    c = c_ref[...]
    ca = c * jax.nn.sigmoid(c)
    o_ref[0] = jnp.dot(ca, w_ref[0], precision=HIGHEST,
                       preferred_element_type=F32) + b_ref[0]


def _ada(c, w_ada, b_ada):
    depth, d, n = w_ada.shape
    bsz = c.shape[0]
    tn = 1024
    return pl.pallas_call(
        _ada_kernel,
        grid=(depth, n // tn),
        in_specs=[
            pl.BlockSpec((bsz, d), lambda l, j: (0, 0)),
            pl.BlockSpec((1, d, tn), lambda l, j: (l, 0, j)),
            pl.BlockSpec((1, 1, tn), lambda l, j: (l, 0, j)),
        ],
        out_specs=pl.BlockSpec((1, bsz, tn), lambda l, j: (l, 0, j)),
        out_shape=jax.ShapeDtypeStruct((depth, bsz, n), F32),
        compiler_params=_cparams(("arbitrary", "arbitrary")),
        name="ada",
    )(c, w_ada, b_ada.reshape(depth, 1, n))


def _rope_tile(acc, cos, sin, scale):
    lane = lax.broadcasted_iota(jnp.int32, cos.shape, 1)
    first_half = (lane % DIFF_HEAD_DIM) < (DIFF_HEAD_DIM // 2)
    outs = []
    for hh in range(DIFF_HEADS):
        t = acc[:, hh * LANES:(hh + 1) * LANES]
        sw = jnp.where(first_half, pltpu.roll(t, LANES - 32, 1), pltpu.roll(t, 32, 1))
        outs.append((t * cos + sw * sin) * scale)
    return jnp.concatenate(outs, axis=1)


def _inproj_kernel(x_ref, g_ref, sc_ref, sh_ref, w_ref, wdt_ref, cos_ref, sin_ref,
                   o_ref, dt_ref, h_scr, *, jq, jk, qscale):
    j = pl.program_id(1)

    @pl.when(j == 0)
    def _():
        xf = x_ref[...]
        ms = jnp.mean(xf * xf, axis=-1, keepdims=True)
        h = xf * lax.rsqrt(ms + EPS) * g_ref[...]
        h = h * (1.0 + sc_ref[0]) + sh_ref[0]
        hb = h.astype(BF16)
        h_scr[...] = hb
        dt_ref[...] = jnp.dot(hb, wdt_ref[...], preferred_element_type=F32)

    acc = jnp.dot(h_scr[...], w_ref[...], preferred_element_type=F32)
    is_rope = jnp.logical_or(j == jq, j == jk)

    @pl.when(jnp.logical_not(is_rope))
    def _():
        o_ref[...] = acc.astype(o_ref.dtype)

    @pl.when(is_rope)
    def _():
        scale = jnp.where(j == jq, qscale, 1.0).astype(F32)
        o_ref[...] = _rope_tile(acc, cos_ref[...], sin_ref[...], scale).astype(o_ref.dtype)


def _inproj(x2, g, sc, sh, w_main, w_dt, cos_t, sin_t, seq):
    t, d = x2.shape
    tm, tn = 1024, 1024
    nps = seq // tm
    qscale = (DIFF_HEAD_DIM ** -0.5) * LOG2E
    kern = functools.partial(_inproj_kernel, jq=COL_Q // tn, jk=COL_K // tn, qscale=qscale)
    return pl.pallas_call(
        kern,
        grid=(t // tm, PROJ_W // tn),
        in_specs=[
            pl.BlockSpec((tm, d), lambda i, j: (i, 0)),
            pl.BlockSpec((1, d), lambda i, j: (0, 0)),
            pl.BlockSpec((1, 1, d), lambda i, j: (i // nps, 0, 0)),
            pl.BlockSpec((1, 1, d), lambda i, j: (i // nps, 0, 0)),
            pl.BlockSpec((d, tn), lambda i, j: (0, j)),
            pl.BlockSpec((d, LANES), lambda i, j: (0, 0)),
            pl.BlockSpec((tm, LANES), lambda i, j: (i % nps, 0)),
            pl.BlockSpec((tm, LANES), lambda i, j: (i % nps, 0)),
        ],
        out_specs=[
            pl.BlockSpec((tm, tn), lambda i, j: (i, j)),
            pl.BlockSpec((tm, LANES), lambda i, j: (i, 0)),
        ],
        out_shape=[
            jax.ShapeDtypeStruct((t, PROJ_W), BF16),
            jax.ShapeDtypeStruct((t, LANES), F32),
        ],
        scratch_shapes=[pltpu.VMEM((tm, d), BF16)],
        compiler_params=_cparams(("arbitrary", "arbitrary")),
        name="inproj",
    )(x2, g, sc, sh, w_main, w_dt, cos_t, sin_t)


def _conv_silu(x_ref, pad_scr, w_ref, b_ref):
    lc = x_ref.shape[0]
    pad_scr[HALO:HALO + lc, :] = x_ref[...].astype(F32)
    acc = b_ref[...] + w_ref[SSM_CONV - 1:SSM_CONV, :] * pad_scr[HALO:HALO + lc, :]
    for k in range(SSM_CONV - 1):
        off = HALO - (SSM_CONV - 1) + k
        acc = acc + w_ref[k:k + 1, :] * pad_scr[off:off + lc, :]
    pad_scr[HALO - (SSM_CONV - 1):HALO, :] = pad_scr[HALO + lc - (SSM_CONV - 1):HALO + lc, :]
    return acc * jax.nn.sigmoid(acc)


def _softplus(x):
    return jnp.maximum(x, 0.0) + jnp.log1p(jnp.exp(-jnp.abs(x)))


def _ssd_kernel(z_ref, xs_ref, bc_ref, dt_ref, cwx_ref, cbx_ref, cwb_ref, cbb_ref,
                dtb_ref, alog_ref, dexp_ref, ng_ref, o_ref,
                padx_scr, padb_scr, state_scr, y_scr):
    lc = SSM_CHUNK
    n = SSM_STATE

    @pl.when(pl.program_id(1) == 0)
    def _():
        state_scr[...] = jnp.zeros_like(state_scr)
        padx_scr[0:HALO, :] = jnp.zeros((HALO, padx_scr.shape[1]), F32)
        padb_scr[0:HALO, :] = jnp.zeros((HALO, padb_scr.shape[1]), F32)

    xs = _conv_silu(xs_ref, padx_scr, cwx_ref, cbx_ref)
    bc = _conv_silu(bc_ref, padb_scr, cwb_ref, cbb_ref)
    xs_b = xs.astype(BF16)

    dtp = _softplus(dt_ref[...] + dtb_ref[...])
    a_step = dtp * (-jnp.exp(alog_ref[...]))
    row_i = lax.broadcasted_iota(jnp.int32, (lc, lc), 0)
    col_i = lax.broadcasted_iota(jnp.int32, (lc, lc), 1)
    tril = row_i >= col_i
    tril_f = tril.astype(F32)
    triu_f = (row_i <= col_i).astype(F32)
    acum = jnp.dot(tril_f, a_step, precision=HIGHEST, preferred_element_type=F32)
    nh8 = SSM_HEADS
    a_step_t = a_step.T[:nh8, :]
    dt_t = dtp.T[:nh8, :]
    acum_t = jnp.dot(a_step_t, triu_f, precision=HIGHEST, preferred_element_type=F32)
    alast_t = acum_t[:, lc - 1:lc]
    w2_t = dt_t * jnp.exp(alast_t - acum_t)
    e1 = jnp.exp(acum)
    dlast = jnp.broadcast_to(jnp.exp(alast_t), (nh8, LANES))

    lane = lax.broadcasted_iota(jnp.int32, (1, LANES), 1)
    lo = lane < SSM_HEAD_DIM
    heads_per_group = SSM_HEADS // SSM_GROUPS
    for g in range(SSM_GROUPS):
        bm = bc[:, g * n:(g + 1) * n]
        cm = bc[:, SSM_GROUPS * n + g * n:SSM_GROUPS * n + (g + 1) * n]
        bm_t = bm.T
        cb = jnp.dot(cm.astype(BF16), bm_t.astype(BF16), preferred_element_type=F32)
        for pp in range(heads_per_group // 2):
            p = g * (heads_per_group // 2) + pp
            cs = slice(p * LANES, (p + 1) * LANES)
            x_pair = xs_b[:, cs]
            s_pair = state_scr[:, cs]
            s_pair_b = s_pair.astype(BF16)
            y = jnp.zeros((lc, LANES), F32)
            s_new = jnp.zeros((n, LANES), F32)
            for e in range(2):
                h = 2 * p + e
                sel = lo if e == 0 else jnp.logical_not(lo)
                x_h = jnp.where(sel, x_pair, jnp.zeros_like(x_pair))
                s_h = jnp.where(sel, s_pair_b, jnp.zeros_like(s_pair_b))
                seg = acum[:, h:h + 1] - acum_t[h:h + 1, :]
                decay = jnp.exp(jnp.where(tril, seg, -1e30))
                m_h = (decay * (cb * dt_t[h:h + 1, :])).astype(BF16)
                y = y + jnp.dot(m_h, x_h, preferred_element_type=F32)
                c_h = (cm * e1[:, h:h + 1]).astype(BF16)
                y = y + jnp.dot(c_h, s_h, preferred_element_type=F32)
                b_h = (bm_t * w2_t[h:h + 1, :]).astype(BF16)
                s_new = s_new + jnp.dot(b_h, x_h, preferred_element_type=F32)
            d_row = jnp.where(lo, dlast[2 * p:2 * p + 1, :], dlast[2 * p + 1:2 * p + 2, :])
            state_scr[:, cs] = s_pair * d_row + s_new
            y_scr[:, cs] = y + dexp_ref[:, cs] * xs[:, cs]

    zf = z_ref[...].astype(F32)
    yg = y_scr[...] * (zf * jax.nn.sigmoid(zf))
    gw = SSM_D_INNER // SSM_GROUPS
    outs = []
    for g in range(SSM_GROUPS):
        blk = yg[:, g * gw:(g + 1) * gw]
        ms = jnp.mean(blk * blk, axis=-1, keepdims=True)
        outs.append(blk * lax.rsqrt(ms + EPS))
    o_ref[...] = (jnp.concatenate(outs, axis=1) * ng_ref[...]).astype(o_ref.dtype)


def _ssd(proj, dt, cwx, cbx, cwb, cbb, dtb, alog, dexp, ng, bsz, seq):
    t = proj.shape[0]
    lc = SSM_CHUNK
    nc = seq // lc
    xw, bw = SSM_D_INNER, 2 * SSM_GROUPS * SSM_STATE
    row = lambda b, c: b * nc + c
    full = lambda b, c: (0, 0)
    return pl.pallas_call(
        _ssd_kernel,
        grid=(bsz, nc),
        in_specs=[
            pl.BlockSpec((lc, xw), lambda b, c: (row(b, c), COL_Z // xw)),
            pl.BlockSpec((lc, xw), lambda b, c: (row(b, c), COL_XS // xw)),
            pl.BlockSpec((lc, bw), lambda b, c: (row(b, c), COL_BC // bw)),
            pl.BlockSpec((lc, LANES), lambda b, c: (row(b, c), 0)),
            pl.BlockSpec((SSM_CONV, xw), full),
            pl.BlockSpec((1, xw), full),
            pl.BlockSpec((SSM_CONV, bw), full),
            pl.BlockSpec((1, bw), full),
            pl.BlockSpec((1, LANES), full),
            pl.BlockSpec((1, LANES), full),
            pl.BlockSpec((1, xw), full),
            pl.BlockSpec((1, xw), full),
        ],
        out_specs=pl.BlockSpec((lc, xw), lambda b, c: (row(b, c), 0)),
        out_shape=jax.ShapeDtypeStruct((t, xw), BF16),
        scratch_shapes=[
            pltpu.VMEM((HALO + lc, xw), F32),
            pltpu.VMEM((HALO + lc, bw), F32),
            pltpu.VMEM((SSM_STATE, xw), F32),
            pltpu.VMEM((lc, xw), F32),
        ],
        compiler_params=_cparams(("arbitrary", "arbitrary")),
        name="ssd",
    )(proj, proj, proj, dt, cwx, cbx, cwb, cbb, dtb, alog, dexp, ng)


def _attn_kernel(q_ref, k_ref, v_ref, lam_ref, sg_ref, o_ref,
                 qt_scr, vt_scr, m_scr, l_scr, acc_scr, sa_scr, sb_scr,
                 *, tq, tk, cw, lam_init):
    qi = pl.program_id(2)
    nkv = vt_scr.shape[0]

    @pl.when(qi == 0)
    def _():
        def tr(c, carry):
            start = pl.multiple_of(c * tk, tk)
            vt_scr[c] = v_ref[pl.ds(start, tk), :].astype(F32).T.astype(BF16)
            return carry
        lax.fori_loop(0, nkv, tr, 0)

    qt = q_ref[...].astype(F32).T
    comp = lax.broadcasted_iota(jnp.int32, qt.shape, 0) < DIFF_HEAD_DIM
    qt_scr[...] = jnp.concatenate([jnp.where(comp, qt, 0.0), jnp.where(comp, 0.0, qt)],
                                  axis=1).astype(BF16)

    m_scr[...] = jnp.full(m_scr.shape, -1e30, F32)
    l_scr[...] = jnp.zeros(l_scr.shape, F32)
    acc_scr[...] = jnp.zeros(acc_scr.shape, F32)

    def scores(j, s_ref):
        start = pl.multiple_of(j * tk, tk)
        s_ref[...] = jnp.dot(k_ref[pl.ds(start, tk), :], qt_scr[...],
                             preferred_element_type=F32)

    def update(j, s_ref, masked):
        vtb = vt_scr[j]
        for c in range(2 * tq // cw):
            cs = slice(c * cw, (c + 1) * cw)
            s = s_ref[:, cs]
            if masked:
                key = lax.broadcasted_iota(jnp.int32, (tk, cw), 0)
                row = lax.broadcasted_iota(jnp.int32, (tk, cw), 1) + (c * cw) % tq
                s = jnp.where(key <= row, s, -1e30)
            m_old = m_scr[:, cs]
            m_new = jnp.maximum(m_old, jnp.max(s, axis=0, keepdims=True))
            alpha = jnp.exp2(m_old - m_new)
            p = jnp.exp2(s - m_new)
            l_scr[:, cs] = alpha * l_scr[:, cs] + jnp.sum(p, axis=0, keepdims=True)
            acc_scr[:, cs] = alpha * acc_scr[:, cs] + jnp.dot(vtb, p.astype(BF16),
                                                              preferred_element_type=F32)
            m_scr[:, cs] = m_new

    scores(0, sa_scr)

    def pair(t, carry):
        j = 2 * t
        scores(j + 1, sb_scr)
        update(j, sa_scr, False)
        scores(j + 2, sa_scr)
        update(j + 1, sb_scr, False)
        return carry

    lax.fori_loop(0, qi // 2, pair, 0)

    @pl.when(qi % 2 == 0)
    def _():
        update(qi, sa_scr, True)

    @pl.when(qi % 2 == 1)
    def _():
        scores(qi, sb_scr)
        update(qi - 1, sa_scr, False)
        update(qi, sb_scr, True)

    lam = (jnp.exp(jnp.sum(lam_ref[0:1, :] * lam_ref[1:2, :], axis=-1, keepdims=True))
           - jnp.exp(jnp.sum(lam_ref[2:3, :] * lam_ref[3:4, :], axis=-1, keepdims=True))
           + lam_init)
    ot = acc_scr[...] / l_scr[...]
    o = (ot[:, :tq] - lam * ot[:, tq:]).T
    ms = jnp.mean(o * o, axis=-1, keepdims=True)
    o = o * lax.rsqrt(ms + EPS) * sg_ref[...] * (1.0 - lam_init)
    o_ref[...] = o.astype(o_ref.dtype)


def _attn(proj, lam4, sg, bsz, seq, lam_init):
    t = proj.shape[0]
    tq = tk = 512
    nq = seq // tq
    kern = functools.partial(_attn_kernel, tq=tq, tk=tk, cw=256, lam_init=lam_init)
    return pl.pallas_call(
        kern,
        grid=(bsz, DIFF_HEADS, nq),
        in_specs=[
            pl.BlockSpec((tq, LANES), lambda b, h, i: (b * nq + i, COL_Q // LANES + h)),
            pl.BlockSpec((seq, LANES), lambda b, h, i: (b, COL_K // LANES + h)),
            pl.BlockSpec((seq, LANES), lambda b, h, i: (b, COL_V // LANES + h)),
            pl.BlockSpec((4, DIFF_HEAD_DIM), lambda b, h, i: (0, 0)),
            pl.BlockSpec((1, LANES), lambda b, h, i: (0, 0)),
        ],
        out_specs=pl.BlockSpec((tq, LANES), lambda b, h, i: (b * nq + i, h)),
        out_shape=jax.ShapeDtypeStruct((t, DIFF_HEADS * LANES), BF16),
        scratch_shapes=[
            pltpu.VMEM((LANES, 2 * tq), BF16),
            pltpu.VMEM((seq // tk, LANES, tk), BF16),
            pltpu.VMEM((1, 2 * tq), F32),
            pltpu.VMEM((1, 2 * tq), F32),
            pltpu.VMEM((LANES, 2 * tq), F32),
            pltpu.VMEM((tk, 2 * tq), F32),
            pltpu.VMEM((tk, 2 * tq), F32),
        ],
        compiler_params=_cparams(("arbitrary", "arbitrary", "arbitrary")),
        name="attn",
    )(proj, proj, proj, lam4, sg)


def _merge_kernel(x_ref, ys_ref, ya_ref, gs_ref, ga_ref, g1_ref, ws_ref, wa_ref, wo_ref, o_ref):
    ys = jnp.dot(ys_ref[...], ws_ref[...], preferred_element_type=F32)
    ya = jnp.dot(ya_ref[...], wa_ref[...], preferred_element_type=F32)
    mixed = (jax.nn.sigmoid(gs_ref[...].astype(F32)) * ys
             + jax.nn.sigmoid(ga_ref[...].astype(F32)) * ya)
    upd = jnp.dot(mixed.astype(BF16), wo_ref[...], preferred_element_type=F32)
    o_ref[...] = x_ref[...] + g1_ref[0] * upd


def _merge(x2, yssm, yatt, proj, g1, ws, wa, wo, seq):
    t, d = x2.shape
    tm = 512
    nps = seq // tm
    const = lambda i: (0, 0)
    return pl.pallas_call(
        _merge_kernel,
        grid=(t // tm,),
        in_specs=[
            pl.BlockSpec((tm, d), lambda i: (i, 0)),
            pl.BlockSpec((tm, SSM_D_INNER), lambda i: (i, 0)),
            pl.BlockSpec((tm, d), lambda i: (i, 0)),
            pl.BlockSpec((tm, d), lambda i: (i, COL_GS // d)),
            pl.BlockSpec((tm, d), lambda i: (i, COL_GA // d)),
            pl.BlockSpec((1, 1, d), lambda i: (i // nps, 0, 0)),
            pl.BlockSpec((SSM_D_INNER, d), const),
            pl.BlockSpec((d, d), const),
            pl.BlockSpec((d, d), const),
        ],
        out_specs=pl.BlockSpec((tm, d), lambda i: (i, 0)),
        out_shape=jax.ShapeDtypeStruct((t, d), F32),
        compiler_params=_cparams(("arbitrary",)),
        name="merge",
    )(x2, yssm, yatt, proj, proj, g1, ws, wa, wo)


def _mlp_kernel(x_ref, g_ref, sc_ref, sh_ref, g2_ref, w1_ref, w2_ref, fg_ref, o_ref, *, final):
    xf = x_ref[...]
    ms = jnp.mean(xf * xf, axis=-1, keepdims=True)
    h = xf * lax.rsqrt(ms + EPS) * g_ref[...]
    h = h * (1.0 + sc_ref[0]) + sh_ref[0]
    u = jnp.dot(h.astype(BF16), w1_ref[...], preferred_element_type=F32)
    u = jnp.square(jnp.maximum(u, 0.0))
    y = xf + g2_ref[0] * jnp.dot(u.astype(BF16), w2_ref[...], preferred_element_type=F32)
    if final:
        ms2 = jnp.mean(y * y, axis=-1, keepdims=True)
        y = y * lax.rsqrt(ms2 + EPS) * fg_ref[...]
    o_ref[...] = y


def _mlp(x2, g, sc, sh, g2, w1, w2, fg, seq, final):
    t, d = x2.shape
    tm = 512
    nps = seq // tm
    const = lambda i: (0, 0)
    per_b = lambda i: (i // nps, 0, 0)
    return pl.pallas_call(
        functools.partial(_mlp_kernel, final=final),
        grid=(t // tm,),
        in_specs=[
            pl.BlockSpec((tm, d), lambda i: (i, 0)),
            pl.BlockSpec((1, d), const),
            pl.BlockSpec((1, 1, d), per_b),
            pl.BlockSpec((1, 1, d), per_b),
            pl.BlockSpec((1, 1, d), per_b),
            pl.BlockSpec((d, D_FF), const),
            pl.BlockSpec((D_FF, d), const),
            pl.BlockSpec((1, d), const),
        ],
        out_specs=pl.BlockSpec((tm, d), lambda i: (i, 0)),
        out_shape=jax.ShapeDtypeStruct((t, d), F32),
        compiler_params=_cparams(("arbitrary",)),
        name="mlp",
    )(x2, g, sc, sh, g2, w1, w2, fg)


def _rope_tables(seq):
    half = DIFF_HEAD_DIM // 2
    inv = 1.0 / (ROPE_THETA ** (jnp.arange(0, DIFF_HEAD_DIM, 2, dtype=F32) / DIFF_HEAD_DIM))
    ang = jnp.arange(seq, dtype=F32)[:, None] * inv[None, :]
    cos, sin = jnp.cos(ang), jnp.sin(ang)
    reps = LANES // half
    cos_t = jnp.tile(cos, (1, reps))
    sin_t = jnp.tile(jnp.concatenate([-sin, sin], axis=1), (1, reps // 2))
    return cos_t, sin_t


def _lambda_init(layer_idx):
    return 0.8 - 0.6 * math.exp(-0.3 * layer_idx)


def kernel(x, c, w_ada, b_ada, norm1_g, w_in, conv_w, conv_b, dt_bias, a_log, d_skip,
           ssm_norm_g, lam_q1, lam_k1, lam_q2, lam_k2, subln_g, w_ssm_out, w_attn_out,
           w_out, norm2_g, w_ff1, w_ff2, final_g):
    bsz, seq, d = x.shape
    depth = w_in.shape[0]
    assert d == D_MODEL and seq % 1024 == 0
    t = bsz * seq
    cos_t, sin_t = _rope_tables(seq)
    ada = _ada(c, w_ada, b_ada)
    x2 = x.reshape(t, d)
    pad_h = LANES - SSM_HEADS
    for l in range(depth):
        mods = [ada[l, :, i * d:(i + 1) * d].reshape(bsz, 1, d) for i in range(N_ADA)]
        sh1, sc1, g1, sh2, sc2, g2 = mods
        w_l = w_in[l]
        w_main = jnp.concatenate([w_l[:, :DT_OFF], w_l[:, DT_OFF + DT_W:]], axis=1).astype(BF16)
        w_dt = jnp.pad(w_l[:, DT_OFF:DT_OFF + DT_W], ((0, 0), (0, pad_h))).astype(BF16)
        proj, dt = _inproj(x2, norm1_g[l].reshape(1, d), sc1, sh1, w_main, w_dt,
                           cos_t, sin_t, seq)
        cw, cb = conv_w[l], conv_b[l].reshape(1, -1)
        yssm = _ssd(
            proj, dt,
            cw[:, :SSM_D_INNER], cb[:, :SSM_D_INNER], cw[:, SSM_D_INNER:], cb[:, SSM_D_INNER:],
            jnp.pad(dt_bias[l], (0, pad_h)).reshape(1, LANES),
            jnp.pad(a_log[l], (0, pad_h)).reshape(1, LANES),
            jnp.repeat(d_skip[l], SSM_HEAD_DIM).reshape(1, SSM_D_INNER),
            ssm_norm_g[l].reshape(1, SSM_D_INNER), bsz, seq)
        lam4 = jnp.stack([lam_q1[l], lam_k1[l], lam_q2[l], lam_k2[l]], axis=0)
        yatt = _attn(proj, lam4, subln_g[l].reshape(1, LANES), bsz, seq, _lambda_init(l))
        x2 = _merge(x2, yssm, yatt, proj, g1, w_ssm_out[l].astype(BF16),
                    w_attn_out[l].astype(BF16), w_out[l].astype(BF16), seq)
        x2 = _mlp(x2, norm2_g[l].reshape(1, d), sc2, sh2, g2, w_ff1[l].astype(BF16),
                  w_ff2[l].astype(BF16), final_g.reshape(1, d), seq, l == depth - 1)
    return x2.reshape(bsz, seq, d)
```

```python
import functools
import math

import jax
import jax.numpy as jnp
from jax import lax
from jax.experimental import pallas as pl
from jax.experimental.pallas import tpu as pltpu

F32 = jnp.float32
BF16 = jnp.bfloat16
HIGHEST = lax.Precision.HIGHEST

D_MODEL = 1024
SSM_D_INNER = 2048
SSM_HEAD_DIM = 64
SSM_HEADS = 32
SSM_GROUPS = 4
SSM_STATE = 128
SSM_CONV = 4
SSM_CHUNK = 128
DIFF_HEADS = 8
DIFF_HEAD_DIM = 64
D_FF = 4096
N_ADA = 6
EPS = 1e-5
ROPE_THETA = 10000.0
DT_OFF = 5120
DT_W = SSM_HEADS

LANES = 128
HALO = 8
VMEM_LIMIT = 56 * 1024 * 1024

COL_Z, COL_XS, COL_BC, COL_Q, COL_K, COL_V, COL_GS, COL_GA = (
    0, 2048, 4096, 5120, 6144, 7168, 8192, 9216)
PROJ_W = 10240

LOG2E = 1.4426950408889634


def _silu(x):
    h = 0.5 * x
    return h + h * jnp.tanh(h)


def _cparams(sem):
    return pltpu.CompilerParams(dimension_semantics=sem, vmem_limit_bytes=VMEM_LIMIT)


def _ada_kernel(c_ref, w_ref, b_ref, o_ref):
    c = c_ref[...]
    ca = c * jax.nn.sigmoid(c)
    o_ref[0] = jnp.dot(ca, w_ref[0], precision=HIGHEST,
                       preferred_element_type=F32) + b_ref[0]


def _ada(c, w_ada, b_ada):
    depth, d, n = w_ada.shape
    bsz = c.shape[0]
    tn = 1024
    return pl.pallas_call(
        _ada_kernel,
        grid=(depth, n // tn),
        in_specs=[
            pl.BlockSpec((bsz, d), lambda l, j: (0, 0)),
            pl.BlockSpec((1, d, tn), lambda l, j: (l, 0, j)),
            pl.BlockSpec((1, 1, tn), lambda l, j: (l, 0, j)),
        ],
        out_specs=pl.BlockSpec((1, bsz, tn), lambda l, j: (l, 0, j)),
        out_shape=jax.ShapeDtypeStruct((depth, bsz, n), F32),
        compiler_params=_cparams(("arbitrary", "arbitrary")),
        name="ada",
    )(c, w_ada, b_ada.reshape(depth, 1, n))


def _rope_tile(acc, cos, sin, scale):
    lane = lax.broadcasted_iota(jnp.int32, cos.shape, 1)
    first_half = (lane % DIFF_HEAD_DIM) < (DIFF_HEAD_DIM // 2)
    outs = []
    for hh in range(acc.shape[1] // LANES):
        t = acc[:, hh * LANES:(hh + 1) * LANES]
        sw = jnp.where(first_half, pltpu.roll(t, LANES - 32, 1), pltpu.roll(t, 32, 1))
        outs.append((t * cos + sw * sin) * scale)
    return jnp.concatenate(outs, axis=1)


def _conv_silu_tile(acc, pad_scr, hist_scr, cw_ref, cb_ref, ccs):
    tm = acc.shape[0]
    nh = SSM_CONV - 1
    pad_scr[HALO:HALO + tm, :] = acc
    pad_scr[HALO - nh:HALO, :] = hist_scr[HALO - nh:HALO, ccs]
    y = cb_ref[:, ccs] + cw_ref[nh:nh + 1, ccs] * acc
    for k in range(nh):
        off = HALO - nh + k
        y = y + cw_ref[k:k + 1, ccs] * pad_scr[off:off + tm, :]
    hist_scr[HALO - nh:HALO, ccs] = pad_scr[HALO + tm - nh:HALO + tm, :]
    return _silu(y)


def _inproj_kernel(x_ref, g_ref, sc_ref, sh_ref, w_ref, wdt_ref, cos_ref, sin_ref,
                   cw_ref, cb_ref, o_ref, dt_ref, pad_scr, hist_scr, *, tn, nps, qscale):
    @pl.when(pl.program_id(0) % nps == 0)
    def _():
        hist_scr[...] = jnp.zeros_like(hist_scr)

    xf = x_ref[...]
    ms = jnp.mean(xf * xf, axis=-1, keepdims=True)
    h = xf * lax.rsqrt(ms + EPS) * g_ref[...]
    h = h * (1.0 + sc_ref[0]) + sh_ref[0]
    hb = h.astype(BF16)
    dt_ref[...] = jnp.dot(hb, wdt_ref[...], preferred_element_type=F32)
    cols = list(range(0, PROJ_W, tn))
    heavy = [c for c in cols if COL_XS <= c < COL_Q]
    light = [c for c in cols if not COL_XS <= c < COL_Q]
    order = []
    while heavy or light:
        if heavy:
            order.append(heavy.pop(0))
        for _ in range(2):
            if light:
                order.append(light.pop(0))
    for col in order:
        cs = slice(col, col + tn)
        acc = jnp.dot(hb, w_ref[:, cs], preferred_element_type=F32)
        if COL_Q <= col < COL_K:
            acc = _rope_tile(acc, cos_ref[...], sin_ref[...], qscale)
        elif COL_K <= col < COL_V:
            acc = _rope_tile(acc, cos_ref[...], sin_ref[...], 1.0)
        elif COL_XS <= col < COL_Q:
            acc = _conv_silu_tile(acc, pad_scr, hist_scr, cw_ref, cb_ref,
                                  slice(col - COL_XS, col - COL_XS + tn))
        o_ref[:, cs] = acc.astype(o_ref.dtype)


def _inproj(x2, g, sc, sh, w_main, w_dt, cos_t, sin_t, conv_w, conv_b, seq):
    t, d = x2.shape
    tm, tn = 512, 256
    nps = seq // tm
    conv_dim = conv_w.shape[1]
    qscale = (DIFF_HEAD_DIM ** -0.5) * LOG2E
    kern = functools.partial(_inproj_kernel, tn=tn, nps=nps, qscale=qscale)
    resident = dict(pipeline_mode=pl.Buffered(1))
    return pl.pallas_call(
        kern,
        grid=(t // tm,),
        in_specs=[
            pl.BlockSpec((tm, d), lambda i: (i, 0)),
            pl.BlockSpec((1, d), lambda i: (0, 0)),
            pl.BlockSpec((1, 1, d), lambda i: (i // nps, 0, 0)),
            pl.BlockSpec((1, 1, d), lambda i: (i // nps, 0, 0)),
            pl.BlockSpec((d, PROJ_W), lambda i: (0, 0), **resident),
            pl.BlockSpec((d, LANES), lambda i: (0, 0), **resident),
            pl.BlockSpec((tm, LANES), lambda i: (i % nps, 0)),
            pl.BlockSpec((tm, LANES), lambda i: (i % nps, 0)),
            pl.BlockSpec((SSM_CONV, conv_dim), lambda i: (0, 0)),
            pl.BlockSpec((1, conv_dim), lambda i: (0, 0)),
        ],
        out_specs=[
            pl.BlockSpec((tm, PROJ_W), lambda i: (i, 0)),
            pl.BlockSpec((tm, LANES), lambda i: (i, 0)),
        ],
        out_shape=[
            jax.ShapeDtypeStruct((t, PROJ_W), BF16),
            jax.ShapeDtypeStruct((t, LANES), F32),
        ],
        scratch_shapes=[
            pltpu.VMEM((HALO + tm, tn), F32),
            pltpu.VMEM((HALO, conv_dim), F32),
        ],
        compiler_params=_cparams(("arbitrary",)),
        name="inproj",
    )(x2, g, sc, sh, w_main, w_dt, cos_t, sin_t, conv_w, conv_b)


def _softplus(x):
    e = jnp.exp(-jnp.abs(x))
    u = 1.0 + e
    log1p_e = jnp.where(u == 1.0, e, jnp.log(u) * (e / (u - 1.0)))
    return jnp.maximum(x, 0.0) + log1p_e


def _split3(x):
    p0 = x.astype(BF16)
    r1 = x - p0.astype(F32)
    p1 = r1.astype(BF16)
    p2 = (r1 - p1.astype(F32)).astype(BF16)
    return p0, p1, p2


def _cumsum_rows(tri, x):
    return sum(jnp.dot(tri, p, preferred_element_type=F32) for p in _split3(x))


def _cumsum_cols(x, tri):
    return sum(jnp.dot(p, tri, preferred_element_type=F32) for p in _split3(x))


def _ssd_kernel(z_ref, xs_ref, bc_ref, dt_ref, dtb_ref, alog_ref, dexp_ref, ng_ref, o_ref,
                state_scr, y_scr):
    lc = SSM_CHUNK
    n = SSM_STATE

    @pl.when(pl.program_id(1) == 0)
    def _():
        state_scr[...] = jnp.zeros_like(state_scr)

    xs_b = xs_ref[...]
    bc = bc_ref[...].astype(F32)

    dtp = _softplus(dt_ref[...] + dtb_ref[...])
    a_step = dtp * (-LOG2E * jnp.exp(alog_ref[...]))
    row_i = lax.broadcasted_iota(jnp.int32, (lc, lc), 0)
    col_i = lax.broadcasted_iota(jnp.int32, (lc, lc), 1)
    tril = row_i >= col_i
    tril_b = jnp.where(tril, 1.0, 0.0).astype(BF16)
    triu_b = jnp.where(row_i <= col_i, 1.0, 0.0).astype(BF16)
    acum = _cumsum_rows(tril_b, a_step)
    nh8 = SSM_HEADS
    a_step_t = a_step.T[:nh8, :]
    dt_t = dtp.T[:nh8, :]
    acum_t = _cumsum_cols(a_step_t, triu_b)
    alast_t = acum_t[:, lc - 1:lc]
    w2_t = dt_t * jnp.exp2(alast_t - acum_t)
    dlast = jnp.broadcast_to(jnp.exp2(alast_t), (nh8, LANES))
    arow_t = acum_t - jnp.log2(dt_t)

    lane = lax.broadcasted_iota(jnp.int32, (1, LANES), 1)
    lo = lane < SSM_HEAD_DIM
    heads_per_group = SSM_HEADS // SSM_GROUPS
    for g in range(SSM_GROUPS):
        bm = bc[:, g * n:(g + 1) * n]
        cm = bc[:, SSM_GROUPS * n + g * n:SSM_GROUPS * n + (g + 1) * n]
        bm_t = bm.T
        cb = jnp.dot(cm.astype(BF16), bm_t.astype(BF16), preferred_element_type=F32)
        for pp in range(heads_per_group // 2):
            p = g * (heads_per_group // 2) + pp
            cs = slice(p * LANES, (p + 1) * LANES)
            x_pair = xs_b[:, cs]
            s_pair = state_scr[:, cs]
            s_pair_b = s_pair.astype(BF16)
            y = jnp.zeros((lc, LANES), F32)
            s_new = jnp.zeros((n, LANES), F32)
            for e in range(2):
                h = 2 * p + e
                sel = lo if e == 0 else jnp.logical_not(lo)
                x_h = jnp.where(sel, x_pair, jnp.zeros_like(x_pair))
                s_h = jnp.where(sel, s_pair_b, jnp.zeros_like(s_pair_b))
                acol = jnp.broadcast_to(acum[:, h:h + 1], (lc, lc))
                seg = acol - arow_t[h:h + 1, :]
                decay_dt = jnp.exp2(jnp.where(tril, seg, -1e30))
                m_h = (decay_dt * cb).astype(BF16)
                y = y + jnp.dot(m_h, x_h, preferred_element_type=F32)
                c_h = (cm * jnp.exp2(acol)).astype(BF16)
                y = y + jnp.dot(c_h, s_h, preferred_element_type=F32)
                b_h = (bm_t * w2_t[h:h + 1, :]).astype(BF16)
                s_new = s_new + jnp.dot(b_h, x_h, preferred_element_type=F32)
            d_row = jnp.where(lo, dlast[2 * p:2 * p + 1, :], dlast[2 * p + 1:2 * p + 2, :])
            state_scr[:, cs] = s_pair * d_row + s_new
            y_scr[:, cs] = y + dexp_ref[:, cs] * x_pair.astype(F32)

    zf = z_ref[...].astype(F32)
    yg = y_scr[...] * _silu(zf)
    gw = SSM_D_INNER // SSM_GROUPS
    outs = []
    for g in range(SSM_GROUPS):
        blk = yg[:, g * gw:(g + 1) * gw]
        ms = jnp.mean(blk * blk, axis=-1, keepdims=True)
        outs.append(blk * lax.rsqrt(ms + EPS))
    o_ref[...] = (jnp.concatenate(outs, axis=1) * ng_ref[...]).astype(o_ref.dtype)


def _ssd(proj, dt, dtb, alog, dexp, ng, bsz, seq):
    t = proj.shape[0]
    lc = SSM_CHUNK
    nc = seq // lc
    xw, bw = SSM_D_INNER, 2 * SSM_GROUPS * SSM_STATE
    row = lambda b, c: b * nc + c
    full = lambda b, c: (0, 0)
    return pl.pallas_call(
        _ssd_kernel,
        grid=(bsz, nc),
        in_specs=[
            pl.BlockSpec((lc, xw), lambda b, c: (row(b, c), COL_Z // xw)),
            pl.BlockSpec((lc, xw), lambda b, c: (row(b, c), COL_XS // xw)),
            pl.BlockSpec((lc, bw), lambda b, c: (row(b, c), COL_BC // bw)),
            pl.BlockSpec((lc, LANES), lambda b, c: (row(b, c), 0)),
            pl.BlockSpec((1, LANES), full),
            pl.BlockSpec((1, LANES), full),
            pl.BlockSpec((1, xw), full),
            pl.BlockSpec((1, xw), full),
        ],
        out_specs=pl.BlockSpec((lc, xw), lambda b, c: (row(b, c), 0)),
        out_shape=jax.ShapeDtypeStruct((t, xw), BF16),
        scratch_shapes=[
            pltpu.VMEM((SSM_STATE, xw), F32),
            pltpu.VMEM((lc, xw), F32),
        ],
        compiler_params=_cparams(("arbitrary", "arbitrary")),
        name="ssd",
    )(proj, proj, proj, dt, dtb, alog, dexp, ng)


def _attn_head_kernel(q_ref, k_ref, v_ref, lam_ref, sg_ref, o_ref,
                      qt_scr, vt_scr, m_scr, l_scr, acc_scr, sa_scr, sb_scr,
                      *, tk, cw, lam_init):
    tq = 2 * tk
    w = 2 * tq
    nq = qt_scr.shape[0]
    nkv = vt_scr.shape[0]
    half_b = (slice(tk, tq), slice(tq + tk, w))

    def prep_v(c, carry):
        start = pl.multiple_of(c * tk, tk)
        vt_scr[c] = v_ref[pl.ds(start, tk), :].astype(F32).T.astype(BF16)
        return carry
    lax.fori_loop(0, nkv, prep_v, 0)

    def prep_q(u, carry):
        start = pl.multiple_of(u * tq, tq)
        qt = q_ref[pl.ds(start, tq), :].astype(F32).T
        comp = lax.broadcasted_iota(jnp.int32, qt.shape, 0) < DIFF_HEAD_DIM
        qt_scr[u] = jnp.concatenate([jnp.where(comp, qt, 0.0), jnp.where(comp, 0.0, qt)],
                                    axis=1).astype(BF16)
        return carry
    lax.fori_loop(0, nq, prep_q, 0)

    def reset_state():
        m_scr[...] = jnp.full(m_scr.shape, -1e30, F32)
        l_scr[...] = jnp.zeros(l_scr.shape, F32)
        acc_scr[...] = jnp.zeros(acc_scr.shape, F32)

    def k_block(j):
        return k_ref[pl.ds(pl.multiple_of(j * tk, tk), tk), :]

    def scores(u, j, s_ref):
        s_ref[...] = jnp.dot(k_block(j), qt_scr[u], preferred_element_type=F32)

    def scores_half_b(u, j, s_ref):
        kb = k_block(j)
        for ls in half_b:
            s_ref[:, ls] = jnp.dot(kb, qt_scr[u, :, ls], preferred_element_type=F32)

    def update_chunk(s_ref, vtb, c, nkeys, row_off):
        cs = slice(c * cw, (c + 1) * cw)
        s = s_ref[0:nkeys, cs]
        if row_off is not None:
            key = lax.broadcasted_iota(jnp.int32, (nkeys, cw), 0)
            row = lax.broadcasted_iota(jnp.int32, (nkeys, cw), 1) + row_off
            s = jnp.where(key <= row, s, -1e30)
        m_old = m_scr[:, cs]
        m_new = jnp.maximum(m_old, jnp.max(s, axis=0, keepdims=True))
        alpha = jnp.exp2(m_old - m_new)
        p = jnp.exp2(s - m_new)
        l_scr[:, cs] = alpha * l_scr[:, cs] + jnp.sum(p, axis=0, keepdims=True)
        acc_scr[:, cs] = alpha * acc_scr[:, cs] + jnp.dot(
            vtb[:, 0:nkeys], p.astype(BF16), preferred_element_type=F32)
        m_scr[:, cs] = m_new

    chunks_per_half = tk // cw

    def update_full(j, s_ref):
        vtb = vt_scr[j]
        for c in range(w // cw):
            update_chunk(s_ref, vtb, c, tk, None)

    def update_diag(j, s_ref, half, other_full):
        vtb = vt_scr[j]
        for c in range(w // cw):
            c_half = (c // chunks_per_half) % 2
            row_off = (c % chunks_per_half) * cw
            if c_half == half:
                update_chunk(s_ref, vtb, c, row_off + cw, row_off)
            elif other_full:
                update_chunk(s_ref, vtb, c, tk, None)

    def finalize(u):
        lam = (jnp.exp(jnp.sum(lam_ref[0:1, :] * lam_ref[1:2, :], axis=-1, keepdims=True))
               - jnp.exp(jnp.sum(lam_ref[2:3, :] * lam_ref[3:4, :], axis=-1, keepdims=True))
               + lam_init)
        ot = acc_scr[...] / l_scr[...]
        o = (ot[:, :tq] - lam * ot[:, tq:]).T
        ms = jnp.mean(o * o, axis=-1, keepdims=True)
        o = o * lax.rsqrt(ms + EPS) * sg_ref[...] * (1.0 - lam_init)
        o_ref[pl.ds(pl.multiple_of(u * tq, tq), tq), :] = o.astype(o_ref.dtype)

    reset_state()
    scores(0, 0, sa_scr)

    def tile(u, carry):
        def pair(t, c2):
            j = 2 * t
            scores(u, j + 1, sb_scr)
            update_full(j, sa_scr)
            scores(u, j + 2, sa_scr)
            update_full(j + 1, sb_scr)
            return c2
        lax.fori_loop(0, u, pair, 0)
        ja = 2 * u
        scores_half_b(u, ja + 1, sb_scr)
        update_diag(ja, sa_scr, 0, True)
        scores(jnp.minimum(u + 1, nq - 1), 0, sa_scr)
        update_diag(ja + 1, sb_scr, 1, False)
        finalize(u)
        reset_state()
        return carry

    lax.fori_loop(0, nq, tile, 0)


def _attn_head(proj, lam4, sg, bsz, seq, lam_init):
    t = proj.shape[0]
    tk = 512
    tq = 2 * tk
    kern = functools.partial(_attn_head_kernel, tk=tk, cw=256, lam_init=lam_init)
    return pl.pallas_call(
        kern,
        grid=(bsz, DIFF_HEADS),
        in_specs=[
            pl.BlockSpec((seq, LANES), lambda b, h: (b, COL_Q // LANES + h)),
            pl.BlockSpec((seq, LANES), lambda b, h: (b, COL_K // LANES + h)),
            pl.BlockSpec((seq, LANES), lambda b, h: (b, COL_V // LANES + h)),
            pl.BlockSpec((4, DIFF_HEAD_DIM), lambda b, h: (0, 0)),
            pl.BlockSpec((1, LANES), lambda b, h: (0, 0)),
        ],
        out_specs=pl.BlockSpec((seq, LANES), lambda b, h: (b, h)),
        out_shape=jax.ShapeDtypeStruct((t, DIFF_HEADS * LANES), BF16),
        scratch_shapes=[
            pltpu.VMEM((seq // tq, LANES, 2 * tq), BF16),
            pltpu.VMEM((seq // tk, LANES, tk), BF16),
            pltpu.VMEM((1, 2 * tq), F32),
            pltpu.VMEM((1, 2 * tq), F32),
            pltpu.VMEM((LANES, 2 * tq), F32),
            pltpu.VMEM((tk, 2 * tq), F32),
            pltpu.VMEM((tk, 2 * tq), F32),
        ],
        compiler_params=_cparams(("arbitrary", "arbitrary")),
        name="attn",
    )(proj, proj, proj, lam4, sg)


def _merge_kernel(x_ref, ys_ref, ya_ref, gs_ref, ga_ref, g1_ref, ws_ref, wa_ref, wo_ref, o_ref):
    ys = jnp.dot(ys_ref[...], ws_ref[...], preferred_element_type=F32)
    ya = jnp.dot(ya_ref[...], wa_ref[...], preferred_element_type=F32)
    mixed = (jax.nn.sigmoid(gs_ref[...].astype(F32)) * ys
             + jax.nn.sigmoid(ga_ref[...].astype(F32)) * ya)
    upd = jnp.dot(mixed.astype(BF16), wo_ref[...], preferred_element_type=F32)
    o_ref[...] = x_ref[...] + g1_ref[0] * upd


def _merge(x2, yssm, yatt, proj, g1, ws, wa, wo, seq):
    t, d = x2.shape
    tm = 512
    nps = seq // tm
    const = lambda i: (0, 0)
    return pl.pallas_call(
        _merge_kernel,
        grid=(t // tm,),
        in_specs=[
            pl.BlockSpec((tm, d), lambda i: (i, 0)),
            pl.BlockSpec((tm, SSM_D_INNER), lambda i: (i, 0)),
            pl.BlockSpec((tm, d), lambda i: (i, 0)),
            pl.BlockSpec((tm, d), lambda i: (i, COL_GS // d)),
            pl.BlockSpec((tm, d), lambda i: (i, COL_GA // d)),
            pl.BlockSpec((1, 1, d), lambda i: (i // nps, 0, 0)),
            pl.BlockSpec((SSM_D_INNER, d), const),
            pl.BlockSpec((d, d), const),
            pl.BlockSpec((d, d), const),
        ],
        out_specs=pl.BlockSpec((tm, d), lambda i: (i, 0)),
        out_shape=jax.ShapeDtypeStruct((t, d), F32),
        compiler_params=_cparams(("arbitrary",)),
        name="merge",
    )(x2, yssm, yatt, proj, proj, g1, ws, wa, wo)


def _mlp_kernel(x_ref, g_ref, sc_ref, sh_ref, g2_ref, w1_ref, w2_ref, fg_ref, o_ref, *, final):
    xf = x_ref[...]
    ms = jnp.mean(xf * xf, axis=-1, keepdims=True)
    h = xf * lax.rsqrt(ms + EPS) * g_ref[...]
    h = h * (1.0 + sc_ref[0]) + sh_ref[0]
    u = jnp.dot(h.astype(BF16), w1_ref[...], preferred_element_type=F32)
    u = jnp.square(jnp.maximum(u, 0.0))
    y = xf + g2_ref[0] * jnp.dot(u.astype(BF16), w2_ref[...], preferred_element_type=F32)
    if final:
        ms2 = jnp.mean(y * y, axis=-1, keepdims=True)
        y = y * lax.rsqrt(ms2 + EPS) * fg_ref[...]
    o_ref[...] = y


def _mlp(x2, g, sc, sh, g2, w1, w2, fg, seq, final):
    t, d = x2.shape
    tm = 512
    nps = seq // tm
    const = lambda i: (0, 0)
    per_b = lambda i: (i // nps, 0, 0)
    return pl.pallas_call(
        functools.partial(_mlp_kernel, final=final),
        grid=(t // tm,),
        in_specs=[
            pl.BlockSpec((tm, d), lambda i: (i, 0)),
            pl.BlockSpec((1, d), const),
            pl.BlockSpec((1, 1, d), per_b),
            pl.BlockSpec((1, 1, d), per_b),
            pl.BlockSpec((1, 1, d), per_b),
            pl.BlockSpec((d, D_FF), const),
            pl.BlockSpec((D_FF, d), const),
            pl.BlockSpec((1, d), const),
        ],
        out_specs=pl.BlockSpec((tm, d), lambda i: (i, 0)),
        out_shape=jax.ShapeDtypeStruct((t, d), F32),
        compiler_params=_cparams(("arbitrary",)),
        name="mlp",
    )(x2, g, sc, sh, g2, w1, w2, fg)


def _rope_tables(seq):
    half = DIFF_HEAD_DIM // 2
    inv = 1.0 / (ROPE_THETA ** (jnp.arange(0, DIFF_HEAD_DIM, 2, dtype=F32) / DIFF_HEAD_DIM))
    ang = jnp.arange(seq, dtype=F32)[:, None] * inv[None, :]
    cos, sin = jnp.cos(ang), jnp.sin(ang)
    reps = LANES // half
    cos_t = jnp.tile(cos, (1, reps))
    sin_t = jnp.tile(jnp.concatenate([-sin, sin], axis=1), (1, reps // 2))
    return cos_t, sin_t


def _lambda_init(layer_idx):
    return 0.8 - 0.6 * math.exp(-0.3 * layer_idx)


def kernel(x, c, w_ada, b_ada, norm1_g, w_in, conv_w, conv_b, dt_bias, a_log, d_skip,
           ssm_norm_g, lam_q1, lam_k1, lam_q2, lam_k2, subln_g, w_ssm_out, w_attn_out,
           w_out, norm2_g, w_ff1, w_ff2, final_g):
    bsz, seq, d = x.shape
    depth = w_in.shape[0]
    assert d == D_MODEL and seq % 1024 == 0
    t = bsz * seq
    cos_t, sin_t = _rope_tables(seq)
    ada = _ada(c, w_ada, b_ada)
    x2 = x.reshape(t, d)
    pad_h = LANES - SSM_HEADS
    for l in range(depth):
        mods = [ada[l, :, i * d:(i + 1) * d].reshape(bsz, 1, d) for i in range(N_ADA)]
        sh1, sc1, g1, sh2, sc2, g2 = mods
        w_l = w_in[l]
        w_main = jnp.concatenate([w_l[:, :DT_OFF], w_l[:, DT_OFF + DT_W:]], axis=1).astype(BF16)
        w_dt = jnp.pad(w_l[:, DT_OFF:DT_OFF + DT_W], ((0, 0), (0, pad_h))).astype(BF16)
        proj, dt = _inproj(x2, norm1_g[l].reshape(1, d), sc1, sh1, w_main, w_dt,
                           cos_t, sin_t, conv_w[l], conv_b[l].reshape(1, -1), seq)
        yssm = _ssd(
            proj, dt,
            jnp.pad(dt_bias[l], (0, pad_h)).reshape(1, LANES),
            jnp.pad(a_log[l], (0, pad_h)).reshape(1, LANES),
            jnp.repeat(d_skip[l], SSM_HEAD_DIM).reshape(1, SSM_D_INNER),
            ssm_norm_g[l].reshape(1, SSM_D_INNER), bsz, seq)
        lam4 = jnp.stack([lam_q1[l], lam_k1[l], lam_q2[l], lam_k2[l]], axis=0)
        yatt = _attn_head(proj, lam4, subln_g[l].reshape(1, LANES), bsz, seq, _lambda_init(l))
        x2 = _merge(x2, yssm, yatt, proj, g1, w_ssm_out[l].astype(BF16),
                    w_attn_out[l].astype(BF16), w_out[l].astype(BF16), seq)
        x2 = _mlp(x2, norm2_g[l].reshape(1, d), sc2, sh2, g2, w_ff1[l].astype(BF16),
                  w_ff2[l].astype(BF16), final_g.reshape(1, d), seq, l == depth - 1)
    return x2.reshape(bsz, seq, d)
```

```python
import functools
import math

import jax
import jax.numpy as jnp
from jax import lax
from jax.experimental import pallas as pl
from jax.experimental.pallas import tpu as pltpu

F32 = jnp.float32
BF16 = jnp.bfloat16
HIGHEST = lax.Precision.HIGHEST

D_MODEL = 1024
SSM_D_INNER = 2048
SSM_HEAD_DIM = 64
SSM_HEADS = 32
SSM_GROUPS = 4
SSM_STATE = 128
SSM_CONV = 4
SSM_CHUNK = 128
DIFF_HEADS = 8
DIFF_HEAD_DIM = 64
D_FF = 4096
N_ADA = 6
EPS = 1e-5
ROPE_THETA = 10000.0
DT_OFF = 5120
DT_W = SSM_HEADS

LANES = 128
HALO = 8
VMEM_LIMIT = 56 * 1024 * 1024

COL_Z, COL_XS, COL_BC, COL_Q, COL_K, COL_V, COL_GS, COL_GA = (
    0, 2048, 4096, 5120, 6144, 7168, 8192, 9216)
PROJ_W = 10240

LOG2E = 1.4426950408889634


def _silu(x):
    h = 0.5 * x
    return h + h * jnp.tanh(h)


def _cparams(sem):
    return pltpu.CompilerParams(dimension_semantics=sem, vmem_limit_bytes=VMEM_LIMIT)


def _ada_kernel(c_ref, w_ref, b_ref, o_ref):
    c = c_ref[...]
    ca = c * jax.nn.sigmoid(c)
    o_ref[0] = jnp.dot(ca, w_ref[0], precision=HIGHEST,
                       preferred_element_type=F32) + b_ref[0]


def _ada(c, w_ada, b_ada):
    depth, d, n = w_ada.shape
    bsz = c.shape[0]
    tn = 1024
    return pl.pallas_call(
        _ada_kernel,
        grid=(depth, n // tn),
        in_specs=[
            pl.BlockSpec((bsz, d), lambda l, j: (0, 0)),
            pl.BlockSpec((1, d, tn), lambda l, j: (l, 0, j)),
            pl.BlockSpec((1, 1, tn), lambda l, j: (l, 0, j)),
        ],
        out_specs=pl.BlockSpec((1, bsz, tn), lambda l, j: (l, 0, j)),
        out_shape=jax.ShapeDtypeStruct((depth, bsz, n), F32),
        compiler_params=_cparams(("arbitrary", "arbitrary")),
        name="ada",
    )(c, w_ada, b_ada.reshape(depth, 1, n))


def _rope_tile(acc, cos, sin, scale):
    lane = lax.broadcasted_iota(jnp.int32, cos.shape, 1)
    first_half = (lane % DIFF_HEAD_DIM) < (DIFF_HEAD_DIM // 2)
    outs = []
    for hh in range(acc.shape[1] // LANES):
        t = acc[:, hh * LANES:(hh + 1) * LANES]
        sw = jnp.where(first_half, pltpu.roll(t, LANES - 32, 1), pltpu.roll(t, 32, 1))
        outs.append((t * cos + sw * sin) * scale)
    return jnp.concatenate(outs, axis=1)


def _conv_silu_tile(acc, pad_scr, hist_scr, cw_ref, cb_ref, ccs):
    tm = acc.shape[0]
    nh = SSM_CONV - 1
    pad_scr[HALO:HALO + tm, :] = acc
    pad_scr[HALO - nh:HALO, :] = hist_scr[HALO - nh:HALO, ccs]
    y = cb_ref[:, ccs] + cw_ref[nh:nh + 1, ccs] * acc
    for k in range(nh):
        off = HALO - nh + k
        y = y + cw_ref[k:k + 1, ccs] * pad_scr[off:off + tm, :]
    hist_scr[HALO - nh:HALO, ccs] = pad_scr[HALO + tm - nh:HALO + tm, :]
    return _silu(y)


def _inproj_kernel(x_ref, g_ref, sc_ref, sh_ref, w_ref, wdt_ref, cos_ref, sin_ref,
                   cw_ref, cb_ref, o_ref, dt_ref, pad_scr, hist_scr, *, tn, nps, qscale):
    @pl.when(pl.program_id(0) % nps == 0)
    def _():
        hist_scr[...] = jnp.zeros_like(hist_scr)

    xf = x_ref[...]
    ms = jnp.mean(xf * xf, axis=-1, keepdims=True)
    h = xf * lax.rsqrt(ms + EPS) * g_ref[...]
    h = h * (1.0 + sc_ref[0]) + sh_ref[0]
    hb = h.astype(BF16)
    dt_ref[...] = jnp.dot(hb, wdt_ref[...], preferred_element_type=F32)
    cols = list(range(0, PROJ_W, tn))
    heavy = [c for c in cols if COL_XS <= c < COL_Q]
    light = [c for c in cols if not COL_XS <= c < COL_Q]
    order = []
    while heavy or light:
        if heavy:
            order.append(heavy.pop(0))
        for _ in range(2):
            if light:
                order.append(light.pop(0))
    for col in order:
        cs = slice(col, col + tn)
        acc = jnp.dot(hb, w_ref[:, cs], preferred_element_type=F32)
        if COL_Q <= col < COL_K:
            acc = _rope_tile(acc, cos_ref[...], sin_ref[...], qscale)
        elif COL_K <= col < COL_V:
            acc = _rope_tile(acc, cos_ref[...], sin_ref[...], 1.0)
        elif COL_XS <= col < COL_Q:
            acc = _conv_silu_tile(acc, pad_scr, hist_scr, cw_ref, cb_ref,
                                  slice(col - COL_XS, col - COL_XS + tn))
        o_ref[:, cs] = acc.astype(o_ref.dtype)


def _inproj(x2, g, sc, sh, w_main, w_dt, cos_t, sin_t, conv_w, conv_b, seq):
    t, d = x2.shape
    tm, tn = 512, 256
    nps = seq // tm
    conv_dim = conv_w.shape[1]
    qscale = (DIFF_HEAD_DIM ** -0.5) * LOG2E
    kern = functools.partial(_inproj_kernel, tn=tn, nps=nps, qscale=qscale)
    resident = dict(pipeline_mode=pl.Buffered(1))
    return pl.pallas_call(
        kern,
        grid=(t // tm,),
        in_specs=[
            pl.BlockSpec((tm, d), lambda i: (i, 0)),
            pl.BlockSpec((1, d), lambda i: (0, 0)),
            pl.BlockSpec((1, 1, d), lambda i: (i // nps, 0, 0)),
            pl.BlockSpec((1, 1, d), lambda i: (i // nps, 0, 0)),
            pl.BlockSpec((d, PROJ_W), lambda i: (0, 0), **resident),
            pl.BlockSpec((d, LANES), lambda i: (0, 0), **resident),
            pl.BlockSpec((tm, LANES), lambda i: (i % nps, 0)),
            pl.BlockSpec((tm, LANES), lambda i: (i % nps, 0)),
            pl.BlockSpec((SSM_CONV, conv_dim), lambda i: (0, 0)),
            pl.BlockSpec((1, conv_dim), lambda i: (0, 0)),
        ],
        out_specs=[
            pl.BlockSpec((tm, PROJ_W), lambda i: (i, 0)),
            pl.BlockSpec((tm, LANES), lambda i: (i, 0)),
        ],
        out_shape=[
            jax.ShapeDtypeStruct((t, PROJ_W), BF16),
            jax.ShapeDtypeStruct((t, LANES), F32),
        ],
        scratch_shapes=[
            pltpu.VMEM((HALO + tm, tn), F32),
            pltpu.VMEM((HALO, conv_dim), F32),
        ],
        compiler_params=_cparams(("arbitrary",)),
        name="inproj",
    )(x2, g, sc, sh, w_main, w_dt, cos_t, sin_t, conv_w, conv_b)


def _softplus(x):
    e = jnp.exp(-jnp.abs(x))
    u = 1.0 + e
    log1p_e = jnp.where(u == 1.0, e, jnp.log(u) * (e / (u - 1.0)))
    return jnp.maximum(x, 0.0) + log1p_e


def _split3(x):
    p0 = x.astype(BF16)
    r1 = x - p0.astype(F32)
    p1 = r1.astype(BF16)
    p2 = (r1 - p1.astype(F32)).astype(BF16)
    return p0, p1, p2


def _cumsum_rows(tri, x):
    return sum(jnp.dot(tri, p, preferred_element_type=F32) for p in _split3(x))


def _cumsum_cols(x, tri):
    return sum(jnp.dot(p, tri, preferred_element_type=F32) for p in _split3(x))


def _ssd_kernel(z_ref, xs_ref, bc_ref, dt_ref, dtb_ref, alog_ref, dexp_ref, ng_ref, o_ref,
                state_scr, y_scr):
    @pl.when(pl.program_id(1) == 0)
    def _():
        state_scr[...] = jnp.zeros_like(state_scr)

    for ci in range(z_ref.shape[0] // SSM_CHUNK):
        rows = pl.ds(ci * SSM_CHUNK, SSM_CHUNK)
        _ssd_chunk(z_ref.at[rows], xs_ref.at[rows], bc_ref.at[rows], dt_ref.at[rows],
                   dtb_ref, alog_ref, dexp_ref, ng_ref, o_ref.at[rows], state_scr,
                   y_scr.at[rows])


def _ssd_chunk(z_ref, xs_ref, bc_ref, dt_ref, dtb_ref, alog_ref, dexp_ref, ng_ref, o_ref,
               state_scr, y_scr):
    lc = SSM_CHUNK
    n = SSM_STATE
    xs_b = xs_ref[...]
    bc = bc_ref[...].astype(F32)

    dtp = _softplus(dt_ref[...] + dtb_ref[...])
    a_step = dtp * (-LOG2E * jnp.exp(alog_ref[...]))
    row_i = lax.broadcasted_iota(jnp.int32, (lc, lc), 0)
    col_i = lax.broadcasted_iota(jnp.int32, (lc, lc), 1)
    tril = row_i >= col_i
    tril_b = jnp.where(tril, 1.0, 0.0).astype(BF16)
    triu_b = jnp.where(row_i <= col_i, 1.0, 0.0).astype(BF16)
    acum = _cumsum_rows(tril_b, a_step)
    nh8 = SSM_HEADS
    a_step_t = a_step.T[:nh8, :]
    dt_t = dtp.T[:nh8, :]
    acum_t = _cumsum_cols(a_step_t, triu_b)
    alast_t = acum_t[:, lc - 1:lc]
    w2_t = dt_t * jnp.exp2(alast_t - acum_t)
    dlast = jnp.broadcast_to(jnp.exp2(alast_t), (nh8, LANES))
    arow_t = acum_t - jnp.log2(dt_t)

    lane = lax.broadcasted_iota(jnp.int32, (1, LANES), 1)
    lo = lane < SSM_HEAD_DIM
    heads_per_group = SSM_HEADS // SSM_GROUPS
    for g in range(SSM_GROUPS):
        bm = bc[:, g * n:(g + 1) * n]
        cm = bc[:, SSM_GROUPS * n + g * n:SSM_GROUPS * n + (g + 1) * n]
        bm_t = bm.T
        cb = jnp.dot(cm.astype(BF16), bm_t.astype(BF16), preferred_element_type=F32)
        for pp in range(heads_per_group // 2):
            p = g * (heads_per_group // 2) + pp
            cs = slice(p * LANES, (p + 1) * LANES)
            x_pair = xs_b[:, cs]
            s_pair = state_scr[:, cs]
            s_pair_b = s_pair.astype(BF16)
            y = jnp.zeros((lc, LANES), F32)
            s_new = jnp.zeros((n, LANES), F32)
            for e in range(2):
                h = 2 * p + e
                sel = lo if e == 0 else jnp.logical_not(lo)
                x_h = jnp.where(sel, x_pair, jnp.zeros_like(x_pair))
                s_h = jnp.where(sel, s_pair_b, jnp.zeros_like(s_pair_b))
                acol = jnp.broadcast_to(acum[:, h:h + 1], (lc, lc))
                seg = acol - arow_t[h:h + 1, :]
                decay_dt = jnp.exp2(jnp.where(tril, seg, -1e30))
                m_h = (decay_dt * cb).astype(BF16)
                y = y + jnp.dot(m_h, x_h, preferred_element_type=F32)
                c_h = (cm * jnp.exp2(acol)).astype(BF16)
                y = y + jnp.dot(c_h, s_h, preferred_element_type=F32)
                b_h = (bm_t * w2_t[h:h + 1, :]).astype(BF16)
                s_new = s_new + jnp.dot(b_h, x_h, preferred_element_type=F32)
            d_row = jnp.where(lo, dlast[2 * p:2 * p + 1, :], dlast[2 * p + 1:2 * p + 2, :])
            state_scr[:, cs] = s_pair * d_row + s_new
            y_scr[:, cs] = y + dexp_ref[:, cs] * x_pair.astype(F32)

    zf = z_ref[...].astype(F32)
    yg = y_scr[...] * _silu(zf)
    gw = SSM_D_INNER // SSM_GROUPS
    outs = []
    for g in range(SSM_GROUPS):
        blk = yg[:, g * gw:(g + 1) * gw]
        ms = jnp.mean(blk * blk, axis=-1, keepdims=True)
        outs.append(blk * lax.rsqrt(ms + EPS))
    o_ref[...] = (jnp.concatenate(outs, axis=1) * ng_ref[...]).astype(o_ref.dtype)


def _ssd(proj, dt, dtb, alog, dexp, ng, bsz, seq):
    t = proj.shape[0]
    lc = 4 * SSM_CHUNK
    nc = seq // lc
    xw, bw = SSM_D_INNER, 2 * SSM_GROUPS * SSM_STATE
    row = lambda b, c: b * nc + c
    full = lambda b, c: (0, 0)
    return pl.pallas_call(
        _ssd_kernel,
        grid=(bsz, nc),
        in_specs=[
            pl.BlockSpec((lc, xw), lambda b, c: (row(b, c), COL_Z // xw)),
            pl.BlockSpec((lc, xw), lambda b, c: (row(b, c), COL_XS // xw)),
            pl.BlockSpec((lc, bw), lambda b, c: (row(b, c), COL_BC // bw)),
            pl.BlockSpec((lc, LANES), lambda b, c: (row(b, c), 0)),
            pl.BlockSpec((1, LANES), full),
            pl.BlockSpec((1, LANES), full),
            pl.BlockSpec((1, xw), full),
            pl.BlockSpec((1, xw), full),
        ],
        out_specs=pl.BlockSpec((lc, xw), lambda b, c: (row(b, c), 0)),
        out_shape=jax.ShapeDtypeStruct((t, xw), BF16),
        scratch_shapes=[
            pltpu.VMEM((SSM_STATE, xw), F32),
            pltpu.VMEM((lc, xw), F32),
        ],
        compiler_params=_cparams(("arbitrary", "arbitrary")),
        name="ssd",
    )(proj, proj, proj, dt, dtb, alog, dexp, ng)


def _attn_head_kernel(q_ref, k_ref, v_ref, lam_ref, sg_ref, o_ref,
                      qt_scr, vt_scr, m_scr, l_scr, acc_scr, sa_scr, sb_scr,
                      *, tk, cw, lam_init):
    tq = 2 * tk
    w = 2 * tq
    nq = qt_scr.shape[0]
    nkv = vt_scr.shape[0]

    def prep_v(c, carry):
        start = pl.multiple_of(c * tk, tk)
        vt_scr[c] = v_ref[pl.ds(start, tk), :].astype(F32).T.astype(BF16)
        return carry
    lax.fori_loop(0, nkv, prep_v, 0)

    def prep_q(u, carry):
        start = pl.multiple_of(u * tq, tq)
        qt = q_ref[pl.ds(start, tq), :].astype(F32).T
        comp = lax.broadcasted_iota(jnp.int32, qt.shape, 0) < DIFF_HEAD_DIM
        qt_scr[u] = jnp.concatenate([jnp.where(comp, qt, 0.0), jnp.where(comp, 0.0, qt)],
                                    axis=1).astype(BF16)
        return carry
    lax.fori_loop(0, nq, prep_q, 0)

    def reset_state():
        m_scr[...] = jnp.full(m_scr.shape, -1e30, F32)
        l_scr[...] = jnp.zeros(l_scr.shape, F32)
        acc_scr[...] = jnp.zeros(acc_scr.shape, F32)

    def k_block(j):
        return k_ref[pl.ds(pl.multiple_of(j * tk, tk), tk), :]

    def scores(u, j, s_ref):
        s_ref[...] = jnp.dot(k_block(j), qt_scr[u], preferred_element_type=F32)

    def score_chunks(u, j, s_ref, half_b_only):
        def one(c):
            cs = slice(c * cw, (c + 1) * cw)
            def run():
                s_ref[:, cs] = jnp.dot(k_block(j), qt_scr[u, :, cs], preferred_element_type=F32)
            return run
        return [one(c) for c in range(w // cw)
                if not half_b_only or (c // chunks_per_half) % 2 == 1]

    def interleave(first, second):
        for i in range(max(len(first), len(second))):
            if i < len(first):
                first[i]()
            if i < len(second):
                second[i]()

    def update_chunk(s_ref, j, c, nkeys, row_off):
        cs = slice(c * cw, (c + 1) * cw)
        s = s_ref[0:nkeys, cs]
        if row_off is not None:
            key = lax.broadcasted_iota(jnp.int32, (nkeys, cw), 0)
            row = lax.broadcasted_iota(jnp.int32, (nkeys, cw), 1) + row_off
            s = jnp.where(key <= row, s, -1e30)
        m_old = m_scr[:, cs]
        m_new = jnp.maximum(m_old, jnp.max(s, axis=0, keepdims=True))
        alpha = jnp.exp2(m_old - m_new)
        p = jnp.exp2(s - m_new)
        l_scr[:, cs] = alpha * l_scr[:, cs] + jnp.sum(p, axis=0, keepdims=True)
        acc_scr[:, cs] = alpha * acc_scr[:, cs] + jnp.dot(
            vt_scr[j, :, 0:nkeys], p.astype(BF16), preferred_element_type=F32)
        m_scr[:, cs] = m_new

    chunks_per_half = tk // cw

    def update_full(j, s_ref):
        return [functools.partial(update_chunk, s_ref, j, c, tk, None) for c in range(w // cw)]

    def update_diag(j, s_ref, half, other_full):
        out = []
        for c in range(w // cw):
            c_half = (c // chunks_per_half) % 2
            row_off = (c % chunks_per_half) * cw
            if c_half == half:
                out.append(functools.partial(update_chunk, s_ref, j, c, row_off + cw, row_off))
            elif other_full:
                out.append(functools.partial(update_chunk, s_ref, j, c, tk, None))
        return out

    def finalize(u):
        lam = (jnp.exp(jnp.sum(lam_ref[0:1, :] * lam_ref[1:2, :], axis=-1, keepdims=True))
               - jnp.exp(jnp.sum(lam_ref[2:3, :] * lam_ref[3:4, :], axis=-1, keepdims=True))
               + lam_init)
        ot = acc_scr[...] / l_scr[...]
        o = (ot[:, :tq] - lam * ot[:, tq:]).T
        ms = jnp.mean(o * o, axis=-1, keepdims=True)
        o = o * lax.rsqrt(ms + EPS) * sg_ref[...] * (1.0 - lam_init)
        o_ref[pl.ds(pl.multiple_of(u * tq, tq), tq), :] = o.astype(o_ref.dtype)

    reset_state()
    scores(0, 0, sa_scr)

    def tile(u, carry):
        def pair(t):
            j = 2 * t
            interleave(score_chunks(u, j + 1, sb_scr, False), update_full(j, sa_scr))
            interleave(score_chunks(u, j + 2, sa_scr, False), update_full(j + 1, sb_scr))

        def two_pairs(t2, c2):
            pair(2 * t2)
            pair(2 * t2 + 1)
            return c2
        lax.fori_loop(0, u // 2, two_pairs, 0)

        @pl.when(u % 2 == 1)
        def _():
            pair(u - 1)

        ja = 2 * u
        interleave(score_chunks(u, ja + 1, sb_scr, True), update_diag(ja, sa_scr, 0, True))
        u_next = jnp.minimum(u + 1, nq - 1)
        interleave(score_chunks(u_next, 0, sa_scr, False), update_diag(ja + 1, sb_scr, 1, False))
        finalize(u)
        reset_state()
        return carry

    lax.fori_loop(0, nq, tile, 0)


def _attn_head(proj, lam4, sg, bsz, seq, lam_init):
    t = proj.shape[0]
    tk = 512
    tq = 2 * tk
    kern = functools.partial(_attn_head_kernel, tk=tk, cw=256, lam_init=lam_init)
    return pl.pallas_call(
        kern,
        grid=(bsz, DIFF_HEADS),
        in_specs=[
            pl.BlockSpec((seq, LANES), lambda b, h: (b, COL_Q // LANES + h)),
            pl.BlockSpec((seq, LANES), lambda b, h: (b, COL_K // LANES + h)),
            pl.BlockSpec((seq, LANES), lambda b, h: (b, COL_V // LANES + h)),
            pl.BlockSpec((4, DIFF_HEAD_DIM), lambda b, h: (0, 0)),
            pl.BlockSpec((1, LANES), lambda b, h: (0, 0)),
        ],
        out_specs=pl.BlockSpec((seq, LANES), lambda b, h: (b, h)),
        out_shape=jax.ShapeDtypeStruct((t, DIFF_HEADS * LANES), BF16),
        scratch_shapes=[
            pltpu.VMEM((seq // tq, LANES, 2 * tq), BF16),
            pltpu.VMEM((seq // tk, LANES, tk), BF16),
            pltpu.VMEM((1, 2 * tq), F32),
            pltpu.VMEM((1, 2 * tq), F32),
            pltpu.VMEM((LANES, 2 * tq), F32),
            pltpu.VMEM((tk, 2 * tq), F32),
            pltpu.VMEM((tk, 2 * tq), F32),
        ],
        compiler_params=_cparams(("arbitrary", "arbitrary")),
        name="attn",
    )(proj, proj, proj, lam4, sg)


def _merge_kernel(x_ref, ys_ref, ya_ref, gs_ref, ga_ref, g1_ref, ws_ref, wa_ref, wo_ref, o_ref):
    ys = jnp.dot(ys_ref[...], ws_ref[...], preferred_element_type=F32)
    ya = jnp.dot(ya_ref[...], wa_ref[...], preferred_element_type=F32)
    mixed = (jax.nn.sigmoid(gs_ref[...].astype(F32)) * ys
             + jax.nn.sigmoid(ga_ref[...].astype(F32)) * ya)
    upd = jnp.dot(mixed.astype(BF16), wo_ref[...], preferred_element_type=F32)
    o_ref[...] = x_ref[...] + g1_ref[0] * upd


def _merge(x2, yssm, yatt, proj, g1, ws, wa, wo, seq):
    t, d = x2.shape
    tm = 512
    nps = seq // tm
    const = lambda i: (0, 0)
    return pl.pallas_call(
        _merge_kernel,
        grid=(t // tm,),
        in_specs=[
            pl.BlockSpec((tm, d), lambda i: (i, 0)),
            pl.BlockSpec((tm, SSM_D_INNER), lambda i: (i, 0)),
            pl.BlockSpec((tm, d), lambda i: (i, 0)),
            pl.BlockSpec((tm, d), lambda i: (i, COL_GS // d)),
            pl.BlockSpec((tm, d), lambda i: (i, COL_GA // d)),
            pl.BlockSpec((1, 1, d), lambda i: (i // nps, 0, 0)),
            pl.BlockSpec((SSM_D_INNER, d), const),
            pl.BlockSpec((d, d), const),
            pl.BlockSpec((d, d), const),
        ],
        out_specs=pl.BlockSpec((tm, d), lambda i: (i, 0)),
        out_shape=jax.ShapeDtypeStruct((t, d), F32),
        compiler_params=_cparams(("arbitrary",)),
        name="merge",
    )(x2, yssm, yatt, proj, proj, g1, ws, wa, wo)


def _mlp_kernel(x_ref, g_ref, sc_ref, sh_ref, g2_ref, w1_ref, w2_ref, fg_ref, o_ref, *, final):
    xf = x_ref[...]
    ms = jnp.mean(xf * xf, axis=-1, keepdims=True)
    h = xf * lax.rsqrt(ms + EPS) * g_ref[...]
    h = h * (1.0 + sc_ref[0]) + sh_ref[0]
    u = jnp.dot(h.astype(BF16), w1_ref[...], preferred_element_type=F32)
    u = jnp.square(jnp.maximum(u, 0.0))
    y = xf + g2_ref[0] * jnp.dot(u.astype(BF16), w2_ref[...], preferred_element_type=F32)
    if final:
        ms2 = jnp.mean(y * y, axis=-1, keepdims=True)
        y = y * lax.rsqrt(ms2 + EPS) * fg_ref[...]
    o_ref[...] = y


def _mlp(x2, g, sc, sh, g2, w1, w2, fg, seq, final):
    t, d = x2.shape
    tm = 512
    nps = seq // tm
    const = lambda i: (0, 0)
    per_b = lambda i: (i // nps, 0, 0)
    return pl.pallas_call(
        functools.partial(_mlp_kernel, final=final),
        grid=(t // tm,),
        in_specs=[
            pl.BlockSpec((tm, d), lambda i: (i, 0)),
            pl.BlockSpec((1, d), const),
            pl.BlockSpec((1, 1, d), per_b),
            pl.BlockSpec((1, 1, d), per_b),
            pl.BlockSpec((1, 1, d), per_b),
            pl.BlockSpec((d, D_FF), const),
            pl.BlockSpec((D_FF, d), const),
            pl.BlockSpec((1, d), const),
        ],
        out_specs=pl.BlockSpec((tm, d), lambda i: (i, 0)),
        out_shape=jax.ShapeDtypeStruct((t, d), F32),
        compiler_params=_cparams(("arbitrary",)),
        name="mlp",
    )(x2, g, sc, sh, g2, w1, w2, fg)


def _rope_tables(seq):
    half = DIFF_HEAD_DIM // 2
    inv = 1.0 / (ROPE_THETA ** (jnp.arange(0, DIFF_HEAD_DIM, 2, dtype=F32) / DIFF_HEAD_DIM))
    ang = jnp.arange(seq, dtype=F32)[:, None] * inv[None, :]
    cos, sin = jnp.cos(ang), jnp.sin(ang)
    reps = LANES // half
    cos_t = jnp.tile(cos, (1, reps))
    sin_t = jnp.tile(jnp.concatenate([-sin, sin], axis=1), (1, reps // 2))
    return cos_t, sin_t


def _lambda_init(layer_idx):
    return 0.8 - 0.6 * math.exp(-0.3 * layer_idx)


def kernel(x, c, w_ada, b_ada, norm1_g, w_in, conv_w, conv_b, dt_bias, a_log, d_skip,
           ssm_norm_g, lam_q1, lam_k1, lam_q2, lam_k2, subln_g, w_ssm_out, w_attn_out,
           w_out, norm2_g, w_ff1, w_ff2, final_g):
    bsz, seq, d = x.shape
    depth = w_in.shape[0]
    assert d == D_MODEL and seq % 1024 == 0
    t = bsz * seq
    cos_t, sin_t = _rope_tables(seq)
    ada = _ada(c, w_ada, b_ada)
    x2 = x.reshape(t, d)
    pad_h = LANES - SSM_HEADS
    for l in range(depth):
        mods = [ada[l, :, i * d:(i + 1) * d].reshape(bsz, 1, d) for i in range(N_ADA)]
        sh1, sc1, g1, sh2, sc2, g2 = mods
        w_l = w_in[l]
        w_main = jnp.concatenate([w_l[:, :DT_OFF], w_l[:, DT_OFF + DT_W:]], axis=1).astype(BF16)
        w_dt = jnp.pad(w_l[:, DT_OFF:DT_OFF + DT_W], ((0, 0), (0, pad_h))).astype(BF16)
        proj, dt = _inproj(x2, norm1_g[l].reshape(1, d), sc1, sh1, w_main, w_dt,
                           cos_t, sin_t, conv_w[l], conv_b[l].reshape(1, -1), seq)
        yssm = _ssd(
            proj, dt,
            jnp.pad(dt_bias[l], (0, pad_h)).reshape(1, LANES),
            jnp.pad(a_log[l], (0, pad_h)).reshape(1, LANES),
            jnp.repeat(d_skip[l], SSM_HEAD_DIM).reshape(1, SSM_D_INNER),
            ssm_norm_g[l].reshape(1, SSM_D_INNER), bsz, seq)
        lam4 = jnp.stack([lam_q1[l], lam_k1[l], lam_q2[l], lam_k2[l]], axis=0)
        yatt = _attn_head(proj, lam4, subln_g[l].reshape(1, LANES), bsz, seq, _lambda_init(l))
        x2 = _merge(x2, yssm, yatt, proj, g1, w_ssm_out[l].astype(BF16),
                    w_attn_out[l].astype(BF16), w_out[l].astype(BF16), seq)
        x2 = _mlp(x2, norm2_g[l].reshape(1, d), sc2, sh2, g2, w_ff1[l].astype(BF16),
                  w_ff2[l].astype(BF16), final_g.reshape(1, d), seq, l == depth - 1)
    return x2.reshape(bsz, seq, d)
```

```python
import functools
import math

import jax
import jax.numpy as jnp
from jax import lax
from jax.experimental import pallas as pl
from jax.experimental.pallas import tpu as pltpu

F32 = jnp.float32
BF16 = jnp.bfloat16
HIGHEST = lax.Precision.HIGHEST

D_MODEL = 1024
SSM_D_INNER = 2048
SSM_HEAD_DIM = 64
SSM_HEADS = 32
SSM_GROUPS = 4
SSM_STATE = 128
SSM_CONV = 4
SSM_CHUNK = 128
DIFF_HEADS = 8
DIFF_HEAD_DIM = 64
D_FF = 4096
N_ADA = 6
EPS = 1e-5
ROPE_THETA = 10000.0
DT_OFF = 5120
DT_W = SSM_HEADS

LANES = 128
HALO = 8
VMEM_LIMIT = 56 * 1024 * 1024

COL_Z, COL_XS, COL_BC, COL_Q, COL_K, COL_V, COL_GS, COL_GA = (
    0, 2048, 4096, 5120, 6144, 7168, 8192, 9216)
PROJ_W = 10240

LOG2E = 1.4426950408889634


def _silu(x):
    h = 0.5 * x
    return h + h * jnp.tanh(h)


def _cparams(sem):
    return pltpu.CompilerParams(dimension_semantics=sem, vmem_limit_bytes=VMEM_LIMIT)


def _ada_kernel(c_ref, w_ref, b_ref, o_ref):
    c = c_ref[...]
    ca = c * jax.nn.sigmoid(c)
    o_ref[0] = jnp.dot(ca, w_ref[0], precision=HIGHEST,
                       preferred_element_type=F32) + b_ref[0]


def _ada(c, w_ada, b_ada):
    depth, d, n = w_ada.shape
    bsz = c.shape[0]
    tn = 1024
    return pl.pallas_call(
        _ada_kernel,
        grid=(depth, n // tn),
        in_specs=[
            pl.BlockSpec((bsz, d), lambda l, j: (0, 0)),
            pl.BlockSpec((1, d, tn), lambda l, j: (l, 0, j)),
            pl.BlockSpec((1, 1, tn), lambda l, j: (l, 0, j)),
        ],
        out_specs=pl.BlockSpec((1, bsz, tn), lambda l, j: (l, 0, j)),
        out_shape=jax.ShapeDtypeStruct((depth, bsz, n), F32),
        compiler_params=_cparams(("arbitrary", "arbitrary")),
        name="ada",
    )(c, w_ada, b_ada.reshape(depth, 1, n))


def _rope_tile(acc, cos, sin, scale):
    lane = lax.broadcasted_iota(jnp.int32, cos.shape, 1)
    first_half = (lane % DIFF_HEAD_DIM) < (DIFF_HEAD_DIM // 2)
    outs = []
    for hh in range(acc.shape[1] // LANES):
        t = acc[:, hh * LANES:(hh + 1) * LANES]
        sw = jnp.where(first_half, pltpu.roll(t, LANES - 32, 1), pltpu.roll(t, 32, 1))
        outs.append((t * cos + sw * sin) * scale)
    return jnp.concatenate(outs, axis=1)


def _conv_silu_tile(pad_scr, hist_scr, cw_ref, cb_ref, ccs):
    tm = pad_scr.shape[0] - HALO
    nh = SSM_CONV - 1
    pad_scr[HALO - nh:HALO, :] = hist_scr[HALO - nh:HALO, ccs]
    y = cb_ref[:, ccs] + cw_ref[nh:nh + 1, ccs] * pad_scr[HALO:HALO + tm, :]
    for k in range(nh):
        off = HALO - nh + k
        y = y + cw_ref[k:k + 1, ccs] * pad_scr[off:off + tm, :]
    hist_scr[HALO - nh:HALO, ccs] = pad_scr[HALO + tm - nh:HALO + tm, :]
    return _silu(y)


def _inproj_kernel(x_ref, g_ref, sc_ref, sh_ref, w_ref, wdt_ref, cos_ref, sin_ref,
                   cw_ref, cb_ref, o_ref, dt_ref, pad_scr, hist_scr, *, tn, nps, qscale):
    @pl.when(pl.program_id(0) % nps == 0)
    def _():
        hist_scr[...] = jnp.zeros_like(hist_scr)

    xf = x_ref[...]
    ms = jnp.mean(xf * xf, axis=-1, keepdims=True)
    h = xf * lax.rsqrt(ms + EPS) * g_ref[...]
    h = h * (1.0 + sc_ref[0]) + sh_ref[0]
    hb = h.astype(BF16)
    dt_ref[...] = jnp.dot(hb, wdt_ref[...], preferred_element_type=F32)
    cols = list(range(0, PROJ_W, tn))
    heavy = [c for c in cols if COL_XS <= c < COL_V]
    light = [c for c in cols if not COL_XS <= c < COL_V]
    order = []
    while heavy or light:
        if heavy:
            order.append(heavy.pop(0))
        if light:
            order.append(light.pop(0))
    tm = x_ref.shape[0]
    pending = None
    n_staged = 0
    for col in order + [None]:
        staged = None
        if col is not None:
            cs = slice(col, col + tn)
            acc = jnp.dot(hb, w_ref[:, cs], preferred_element_type=F32)
            if COL_XS <= col < COL_V:
                slot = n_staged % pad_scr.shape[0]
                n_staged += 1
                pad_scr[slot, HALO:HALO + tm, :] = acc
                staged = (slot, col)
            else:
                o_ref[:, cs] = acc.astype(o_ref.dtype)
        if pending is not None:
            slot, pcol = pending
            if pcol < COL_Q:
                y = _conv_silu_tile(pad_scr.at[slot], hist_scr, cw_ref, cb_ref,
                                    slice(pcol - COL_XS, pcol - COL_XS + tn))
            else:
                y = _rope_tile(pad_scr[slot, HALO:HALO + tm, :], cos_ref[...], sin_ref[...],
                               qscale if pcol < COL_K else 1.0)
            o_ref[:, pcol:pcol + tn] = y.astype(o_ref.dtype)
        pending = staged


def _inproj(x2, g, sc, sh, w_main, w_dt, cos_t, sin_t, conv_w, conv_b, seq):
    t, d = x2.shape
    tm, tn = 512, 256
    nps = seq // tm
    conv_dim = conv_w.shape[1]
    qscale = (DIFF_HEAD_DIM ** -0.5) * LOG2E
    kern = functools.partial(_inproj_kernel, tn=tn, nps=nps, qscale=qscale)
    resident = dict(pipeline_mode=pl.Buffered(1))
    return pl.pallas_call(
        kern,
        grid=(t // tm,),
        in_specs=[
            pl.BlockSpec((tm, d), lambda i: (i, 0)),
            pl.BlockSpec((1, d), lambda i: (0, 0)),
            pl.BlockSpec((1, 1, d), lambda i: (i // nps, 0, 0)),
            pl.BlockSpec((1, 1, d), lambda i: (i // nps, 0, 0)),
            pl.BlockSpec((d, PROJ_W), lambda i: (0, 0), **resident),
            pl.BlockSpec((d, LANES), lambda i: (0, 0), **resident),
            pl.BlockSpec((tm, LANES), lambda i: (i % nps, 0)),
            pl.BlockSpec((tm, LANES), lambda i: (i % nps, 0)),
            pl.BlockSpec((SSM_CONV, conv_dim), lambda i: (0, 0)),
            pl.BlockSpec((1, conv_dim), lambda i: (0, 0)),
        ],
        out_specs=[
            pl.BlockSpec((tm, PROJ_W), lambda i: (i, 0)),
            pl.BlockSpec((tm, LANES), lambda i: (i, 0)),
        ],
        out_shape=[
            jax.ShapeDtypeStruct((t, PROJ_W), BF16),
            jax.ShapeDtypeStruct((t, LANES), F32),
        ],
        scratch_shapes=[
            pltpu.VMEM((4, HALO + tm, tn), F32),
            pltpu.VMEM((HALO, conv_dim), F32),
        ],
        compiler_params=_cparams(("arbitrary",)),
        name="inproj",
    )(x2, g, sc, sh, w_main, w_dt, cos_t, sin_t, conv_w, conv_b)


def _softplus(x):
    e = jnp.exp(-jnp.abs(x))
    u = 1.0 + e
    log1p_e = jnp.where(u == 1.0, e, jnp.log(u) * (e / (u - 1.0)))
    return jnp.maximum(x, 0.0) + log1p_e


def _split3(x):
    p0 = x.astype(BF16)
    r1 = x - p0.astype(F32)
    p1 = r1.astype(BF16)
    p2 = (r1 - p1.astype(F32)).astype(BF16)
    return p0, p1, p2


def _cumsum_rows(tri, x):
    return sum(jnp.dot(tri, p, preferred_element_type=F32) for p in _split3(x))


def _cumsum_cols(x, tri):
    return sum(jnp.dot(p, tri, preferred_element_type=F32) for p in _split3(x))


def _ssd_kernel(z_ref, xs_ref, bc_ref, dt_ref, dtb_ref, alog_ref, dexp_ref, ng_ref, o_ref,
                state_scr, y_scr):
    @pl.when(pl.program_id(1) == 0)
    def _():
        state_scr[...] = jnp.zeros_like(state_scr)

    for ci in range(z_ref.shape[0] // SSM_CHUNK):
        rows = pl.ds(ci * SSM_CHUNK, SSM_CHUNK)
        _ssd_chunk(z_ref.at[rows], xs_ref.at[rows], bc_ref.at[rows], dt_ref.at[rows],
                   dtb_ref, alog_ref, dexp_ref, ng_ref, o_ref.at[rows], state_scr,
                   y_scr.at[rows])


def _ssd_chunk(z_ref, xs_ref, bc_ref, dt_ref, dtb_ref, alog_ref, dexp_ref, ng_ref, o_ref,
               state_scr, y_scr):
    lc = SSM_CHUNK
    n = SSM_STATE
    xs_b = xs_ref[...]
    bc = bc_ref[...].astype(F32)

    dtp = _softplus(dt_ref[...] + dtb_ref[...])
    a_step = dtp * (-LOG2E * jnp.exp(alog_ref[...]))
    row_i = lax.broadcasted_iota(jnp.int32, (lc, lc), 0)
    col_i = lax.broadcasted_iota(jnp.int32, (lc, lc), 1)
    tril = row_i >= col_i
    tril_b = jnp.where(tril, 1.0, 0.0).astype(BF16)
    triu_b = jnp.where(row_i <= col_i, 1.0, 0.0).astype(BF16)
    acum = _cumsum_rows(tril_b, a_step)
    nh8 = SSM_HEADS
    a_step_t = a_step.T[:nh8, :]
    dt_t = dtp.T[:nh8, :]
    acum_t = _cumsum_cols(a_step_t, triu_b)
    alast_t = acum_t[:, lc - 1:lc]
    w2_t = dt_t * jnp.exp2(alast_t - acum_t)
    dlast = jnp.broadcast_to(jnp.exp2(alast_t), (nh8, LANES))
    arow_t = acum_t - jnp.log2(dt_t)

    lane = lax.broadcasted_iota(jnp.int32, (1, LANES), 1)
    lo = lane < SSM_HEAD_DIM
    heads_per_group = SSM_HEADS // SSM_GROUPS
    for g in range(SSM_GROUPS):
        bm = bc[:, g * n:(g + 1) * n]
        cm = bc[:, SSM_GROUPS * n + g * n:SSM_GROUPS * n + (g + 1) * n]
        bm_t = bm.T
        cb = jnp.dot(cm.astype(BF16), bm_t.astype(BF16), preferred_element_type=F32)
        for pp in range(heads_per_group // 2):
            p = g * (heads_per_group // 2) + pp
            cs = slice(p * LANES, (p + 1) * LANES)
            x_pair = xs_b[:, cs]
            s_pair = state_scr[:, cs]
            s_pair_b = s_pair.astype(BF16)
            y = jnp.zeros((lc, LANES), F32)
            s_new = jnp.zeros((n, LANES), F32)
            for e in range(2):
                h = 2 * p + e
                sel = lo if e == 0 else jnp.logical_not(lo)
                x_h = jnp.where(sel, x_pair, jnp.zeros_like(x_pair))
                s_h = jnp.where(sel, s_pair_b, jnp.zeros_like(s_pair_b))
                acol = jnp.broadcast_to(acum[:, h:h + 1], (lc, lc))
                seg = acol - arow_t[h:h + 1, :]
                decay_dt = jnp.exp2(jnp.where(tril, seg, -1e30))
                m_h = (decay_dt * cb).astype(BF16)
                y = y + jnp.dot(m_h, x_h, preferred_element_type=F32)
                c_h = (cm * jnp.exp2(acol)).astype(BF16)
                y = y + jnp.dot(c_h, s_h, preferred_element_type=F32)
                b_h = (bm_t * w2_t[h:h + 1, :]).astype(BF16)
                s_new = s_new + jnp.dot(b_h, x_h, preferred_element_type=F32)
            d_row = jnp.where(lo, dlast[2 * p:2 * p + 1, :], dlast[2 * p + 1:2 * p + 2, :])
            state_scr[:, cs] = s_pair * d_row + s_new
            y_scr[:, cs] = y + dexp_ref[:, cs] * x_pair.astype(F32)

    zf = z_ref[...].astype(F32)
    yg = y_scr[...] * _silu(zf)
    gw = SSM_D_INNER // SSM_GROUPS
    outs = []
    for g in range(SSM_GROUPS):
        blk = yg[:, g * gw:(g + 1) * gw]
        ms = jnp.mean(blk * blk, axis=-1, keepdims=True)
        outs.append(blk * lax.rsqrt(ms + EPS))
    o_ref[...] = (jnp.concatenate(outs, axis=1) * ng_ref[...]).astype(o_ref.dtype)


def _ssd(proj, dt, dtb, alog, dexp, ng, bsz, seq):
    t = proj.shape[0]
    lc = 4 * SSM_CHUNK
    nc = seq // lc
    xw, bw = SSM_D_INNER, 2 * SSM_GROUPS * SSM_STATE
    row = lambda b, c: b * nc + c
    full = lambda b, c: (0, 0)
    return pl.pallas_call(
        _ssd_kernel,
        grid=(bsz, nc),
        in_specs=[
            pl.BlockSpec((lc, xw), lambda b, c: (row(b, c), COL_Z // xw)),
            pl.BlockSpec((lc, xw), lambda b, c: (row(b, c), COL_XS // xw)),
            pl.BlockSpec((lc, bw), lambda b, c: (row(b, c), COL_BC // bw)),
            pl.BlockSpec((lc, LANES), lambda b, c: (row(b, c), 0)),
            pl.BlockSpec((1, LANES), full),
            pl.BlockSpec((1, LANES), full),
            pl.BlockSpec((1, xw), full),
            pl.BlockSpec((1, xw), full),
        ],
        out_specs=pl.BlockSpec((lc, xw), lambda b, c: (row(b, c), 0)),
        out_shape=jax.ShapeDtypeStruct((t, xw), BF16),
        scratch_shapes=[
            pltpu.VMEM((SSM_STATE, xw), F32),
            pltpu.VMEM((lc, xw), F32),
        ],
        compiler_params=_cparams(("arbitrary", "arbitrary")),
        name="ssd",
    )(proj, proj, proj, dt, dtb, alog, dexp, ng)


def _attn_head_kernel(q_ref, k_ref, v_ref, lam_ref, sg_ref, o_ref,
                      qt_scr, vt_scr, m_scr, l_scr, acc_scr, sa_scr, sb_scr,
                      *, tk, cw, lam_init):
    tq = 2 * tk
    w = 2 * tq
    nq = qt_scr.shape[0]
    nkv = vt_scr.shape[0]

    def prep_v(c, carry):
        start = pl.multiple_of(c * tk, tk)
        vt_scr[c] = v_ref[pl.ds(start, tk), :].astype(F32).T.astype(BF16)
        return carry
    lax.fori_loop(0, nkv, prep_v, 0)

    def prep_q(u, carry):
        start = pl.multiple_of(u * tq, tq)
        qt = q_ref[pl.ds(start, tq), :].astype(F32).T
        comp = lax.broadcasted_iota(jnp.int32, qt.shape, 0) < DIFF_HEAD_DIM
        qt_scr[u] = jnp.concatenate([jnp.where(comp, qt, 0.0), jnp.where(comp, 0.0, qt)],
                                    axis=1).astype(BF16)
        return carry
    lax.fori_loop(0, nq, prep_q, 0)

    def reset_state():
        m_scr[...] = jnp.full(m_scr.shape, -1e30, F32)
        l_scr[...] = jnp.zeros(l_scr.shape, F32)
        acc_scr[...] = jnp.zeros(acc_scr.shape, F32)

    def k_block(j):
        return k_ref[pl.ds(pl.multiple_of(j * tk, tk), tk), :]

    def scores(u, j, s_ref):
        s_ref[...] = jnp.dot(k_block(j), qt_scr[u], preferred_element_type=F32)

    def score_chunks(u, j, s_ref, half_b_only):
        def one(c):
            cs = slice(c * cw, (c + 1) * cw)
            def run():
                s_ref[:, cs] = jnp.dot(k_block(j), qt_scr[u, :, cs], preferred_element_type=F32)
            return run
        return [one(c) for c in range(w // cw)
                if not half_b_only or (c // chunks_per_half) % 2 == 1]

    def interleave(first, second):
        for i in range(max(len(first), len(second))):
            if i < len(first):
                first[i]()
            if i < len(second):
                second[i]()

    def update_chunk(s_ref, j, c, nkeys, row_off):
        cs = slice(c * cw, (c + 1) * cw)
        s = s_ref[0:nkeys, cs]
        if row_off is not None:
            key = lax.broadcasted_iota(jnp.int32, (nkeys, cw), 0)
            row = lax.broadcasted_iota(jnp.int32, (nkeys, cw), 1) + row_off
            s = jnp.where(key <= row, s, -1e30)
        m_old = m_scr[:, cs]
        m_new = jnp.maximum(m_old, jnp.max(s, axis=0, keepdims=True))
        alpha = jnp.exp2(m_old - m_new)
        p = jnp.exp2(s - m_new)
        l_scr[:, cs] = alpha * l_scr[:, cs] + jnp.sum(p, axis=0, keepdims=True)
        acc_scr[:, cs] = alpha * acc_scr[:, cs] + jnp.dot(
            vt_scr[j, :, 0:nkeys], p.astype(BF16), preferred_element_type=F32)
        m_scr[:, cs] = m_new

    chunks_per_half = tk // cw

    def update_full(j, s_ref):
        return [functools.partial(update_chunk, s_ref, j, c, tk, None) for c in range(w // cw)]

    def update_diag(j, s_ref, half, other_full):
        out = []
        for c in range(w // cw):
            c_half = (c // chunks_per_half) % 2
            row_off = (c % chunks_per_half) * cw
            if c_half == half:
                out.append(functools.partial(update_chunk, s_ref, j, c, row_off + cw, row_off))
            elif other_full:
                out.append(functools.partial(update_chunk, s_ref, j, c, tk, None))
        return out

    def finalize(u):
        lam = (jnp.exp(jnp.sum(lam_ref[0:1, :] * lam_ref[1:2, :], axis=-1, keepdims=True))
               - jnp.exp(jnp.sum(lam_ref[2:3, :] * lam_ref[3:4, :], axis=-1, keepdims=True))
               + lam_init)
        ot = acc_scr[...] / l_scr[...]
        o = (ot[:, :tq] - lam * ot[:, tq:]).T
        ms = jnp.mean(o * o, axis=-1, keepdims=True)
        o = o * lax.rsqrt(ms + EPS) * sg_ref[...] * (1.0 - lam_init)
        o_ref[pl.ds(pl.multiple_of(u * tq, tq), tq), :] = o.astype(o_ref.dtype)

    reset_state()
    scores(0, 0, sa_scr)

    def tile(u, carry):
        def pair(t):
            j = 2 * t
            interleave(score_chunks(u, j + 1, sb_scr, False), update_full(j, sa_scr))
            interleave(score_chunks(u, j + 2, sa_scr, False), update_full(j + 1, sb_scr))

        def two_pairs(t2, c2):
            pair(2 * t2)
            pair(2 * t2 + 1)
            return c2
        lax.fori_loop(0, u // 2, two_pairs, 0)

        @pl.when(u % 2 == 1)
        def _():
            pair(u - 1)

        ja = 2 * u
        interleave(score_chunks(u, ja + 1, sb_scr, True), update_diag(ja, sa_scr, 0, True))
        u_next = jnp.minimum(u + 1, nq - 1)
        interleave(score_chunks(u_next, 0, sa_scr, False), update_diag(ja + 1, sb_scr, 1, False))
        finalize(u)
        reset_state()
        return carry

    lax.fori_loop(0, nq, tile, 0)


def _attn_head(proj, lam4, sg, bsz, seq, lam_init):
    t = proj.shape[0]
    tk = 512
    tq = 2 * tk
    kern = functools.partial(_attn_head_kernel, tk=tk, cw=256, lam_init=lam_init)
    return pl.pallas_call(
        kern,
        grid=(bsz, DIFF_HEADS),
        in_specs=[
            pl.BlockSpec((seq, LANES), lambda b, h: (b, COL_Q // LANES + h)),
            pl.BlockSpec((seq, LANES), lambda b, h: (b, COL_K // LANES + h)),
            pl.BlockSpec((seq, LANES), lambda b, h: (b, COL_V // LANES + h)),
            pl.BlockSpec((4, DIFF_HEAD_DIM), lambda b, h: (0, 0)),
            pl.BlockSpec((1, LANES), lambda b, h: (0, 0)),
        ],
        out_specs=pl.BlockSpec((seq, LANES), lambda b, h: (b, h)),
        out_shape=jax.ShapeDtypeStruct((t, DIFF_HEADS * LANES), BF16),
        scratch_shapes=[
            pltpu.VMEM((seq // tq, LANES, 2 * tq), BF16),
            pltpu.VMEM((seq // tk, LANES, tk), BF16),
            pltpu.VMEM((1, 2 * tq), F32),
            pltpu.VMEM((1, 2 * tq), F32),
            pltpu.VMEM((LANES, 2 * tq), F32),
            pltpu.VMEM((tk, 2 * tq), F32),
            pltpu.VMEM((tk, 2 * tq), F32),
        ],
        compiler_params=_cparams(("arbitrary", "arbitrary")),
        name="attn",
    )(proj, proj, proj, lam4, sg)


def _merge_kernel(x_ref, ys_ref, ya_ref, gs_ref, ga_ref, g1_ref, ws_ref, wa_ref, wo_ref, o_ref):
    ys = jnp.dot(ys_ref[...], ws_ref[...], preferred_element_type=F32)
    ya = jnp.dot(ya_ref[...], wa_ref[...], preferred_element_type=F32)
    mixed = (jax.nn.sigmoid(gs_ref[...].astype(F32)) * ys
             + jax.nn.sigmoid(ga_ref[...].astype(F32)) * ya)
    upd = jnp.dot(mixed.astype(BF16), wo_ref[...], preferred_element_type=F32)
    o_ref[...] = x_ref[...] + g1_ref[0] * upd


def _merge(x2, yssm, yatt, proj, g1, ws, wa, wo, seq):
    t, d = x2.shape
    tm = 512
    nps = seq // tm
    const = lambda i: (0, 0)
    return pl.pallas_call(
        _merge_kernel,
        grid=(t // tm,),
        in_specs=[
            pl.BlockSpec((tm, d), lambda i: (i, 0)),
            pl.BlockSpec((tm, SSM_D_INNER), lambda i: (i, 0)),
            pl.BlockSpec((tm, d), lambda i: (i, 0)),
            pl.BlockSpec((tm, d), lambda i: (i, COL_GS // d)),
            pl.BlockSpec((tm, d), lambda i: (i, COL_GA // d)),
            pl.BlockSpec((1, 1, d), lambda i: (i // nps, 0, 0)),
            pl.BlockSpec((SSM_D_INNER, d), const),
            pl.BlockSpec((d, d), const),
            pl.BlockSpec((d, d), const),
        ],
        out_specs=pl.BlockSpec((tm, d), lambda i: (i, 0)),
        out_shape=jax.ShapeDtypeStruct((t, d), F32),
        compiler_params=_cparams(("arbitrary",)),
        name="merge",
    )(x2, yssm, yatt, proj, proj, g1, ws, wa, wo)


def _mlp_kernel(x_ref, g_ref, sc_ref, sh_ref, g2_ref, w1_ref, w2_ref, fg_ref, o_ref, *, final):
    xf = x_ref[...]
    ms = jnp.mean(xf * xf, axis=-1, keepdims=True)
    h = xf * lax.rsqrt(ms + EPS) * g_ref[...]
    h = h * (1.0 + sc_ref[0]) + sh_ref[0]
    u = jnp.dot(h.astype(BF16), w1_ref[...], preferred_element_type=F32)
    u = jnp.square(jnp.maximum(u, 0.0))
    y = xf + g2_ref[0] * jnp.dot(u.astype(BF16), w2_ref[...], preferred_element_type=F32)
    if final:
        ms2 = jnp.mean(y * y, axis=-1, keepdims=True)
        y = y * lax.rsqrt(ms2 + EPS) * fg_ref[...]
    o_ref[...] = y


def _mlp(x2, g, sc, sh, g2, w1, w2, fg, seq, final):
    t, d = x2.shape
    tm = 512
    nps = seq // tm
    const = lambda i: (0, 0)
    per_b = lambda i: (i // nps, 0, 0)
    return pl.pallas_call(
        functools.partial(_mlp_kernel, final=final),
        grid=(t // tm,),
        in_specs=[
            pl.BlockSpec((tm, d), lambda i: (i, 0)),
            pl.BlockSpec((1, d), const),
            pl.BlockSpec((1, 1, d), per_b),
            pl.BlockSpec((1, 1, d), per_b),
            pl.BlockSpec((1, 1, d), per_b),
            pl.BlockSpec((d, D_FF), const),
            pl.BlockSpec((D_FF, d), const),
            pl.BlockSpec((1, d), const),
        ],
        out_specs=pl.BlockSpec((tm, d), lambda i: (i, 0)),
        out_shape=jax.ShapeDtypeStruct((t, d), F32),
        compiler_params=_cparams(("arbitrary",)),
        name="mlp",
    )(x2, g, sc, sh, g2, w1, w2, fg)


def _rope_tables(seq):
    half = DIFF_HEAD_DIM // 2
    inv = 1.0 / (ROPE_THETA ** (jnp.arange(0, DIFF_HEAD_DIM, 2, dtype=F32) / DIFF_HEAD_DIM))
    ang = jnp.arange(seq, dtype=F32)[:, None] * inv[None, :]
    cos, sin = jnp.cos(ang), jnp.sin(ang)
    reps = LANES // half
    cos_t = jnp.tile(cos, (1, reps))
    sin_t = jnp.tile(jnp.concatenate([-sin, sin], axis=1), (1, reps // 2))
    return cos_t, sin_t


def _lambda_init(layer_idx):
    return 0.8 - 0.6 * math.exp(-0.3 * layer_idx)


def kernel(x, c, w_ada, b_ada, norm1_g, w_in, conv_w, conv_b, dt_bias, a_log, d_skip,
           ssm_norm_g, lam_q1, lam_k1, lam_q2, lam_k2, subln_g, w_ssm_out, w_attn_out,
           w_out, norm2_g, w_ff1, w_ff2, final_g):
    bsz, seq, d = x.shape
    depth = w_in.shape[0]
    assert d == D_MODEL and seq % 1024 == 0
    t = bsz * seq
    cos_t, sin_t = _rope_tables(seq)
    ada = _ada(c, w_ada, b_ada)
    x2 = x.reshape(t, d)
    pad_h = LANES - SSM_HEADS
    for l in range(depth):
        mods = [ada[l, :, i * d:(i + 1) * d].reshape(bsz, 1, d) for i in range(N_ADA)]
        sh1, sc1, g1, sh2, sc2, g2 = mods
        w_l = w_in[l]
        w_main = jnp.concatenate([w_l[:, :DT_OFF], w_l[:, DT_OFF + DT_W:]], axis=1).astype(BF16)
        w_dt = jnp.pad(w_l[:, DT_OFF:DT_OFF + DT_W], ((0, 0), (0, pad_h))).astype(BF16)
        proj, dt = _inproj(x2, norm1_g[l].reshape(1, d), sc1, sh1, w_main, w_dt,
                           cos_t, sin_t, conv_w[l], conv_b[l].reshape(1, -1), seq)
        yssm = _ssd(
            proj, dt,
            jnp.pad(dt_bias[l], (0, pad_h)).reshape(1, LANES),
            jnp.pad(a_log[l], (0, pad_h)).reshape(1, LANES),
            jnp.repeat(d_skip[l], SSM_HEAD_DIM).reshape(1, SSM_D_INNER),
            ssm_norm_g[l].reshape(1, SSM_D_INNER), bsz, seq)
        lam4 = jnp.stack([lam_q1[l], lam_k1[l], lam_q2[l], lam_k2[l]], axis=0)
        yatt = _attn_head(proj, lam4, subln_g[l].reshape(1, LANES), bsz, seq, _lambda_init(l))
        x2 = _merge(x2, yssm, yatt, proj, g1, w_ssm_out[l].astype(BF16),
                    w_attn_out[l].astype(BF16), w_out[l].astype(BF16), seq)
        x2 = _mlp(x2, norm2_g[l].reshape(1, d), sc2, sh2, g2, w_ff1[l].astype(BF16),
                  w_ff2[l].astype(BF16), final_g.reshape(1, d), seq, l == depth - 1)
    return x2.reshape(bsz, seq, d)
```

```python
import functools
import math

import jax
import jax.numpy as jnp
from jax import lax
from jax.experimental import pallas as pl
from jax.experimental.pallas import tpu as pltpu

F32 = jnp.float32
BF16 = jnp.bfloat16
HIGHEST = lax.Precision.HIGHEST

D_MODEL = 1024
SSM_D_INNER = 2048
SSM_HEAD_DIM = 64
SSM_HEADS = 32
SSM_GROUPS = 4
SSM_STATE = 128
SSM_CONV = 4
SSM_CHUNK = 128
DIFF_HEADS = 8
DIFF_HEAD_DIM = 64
D_FF = 4096
N_ADA = 6
EPS = 1e-5
ROPE_THETA = 10000.0
DT_OFF = 5120
DT_W = SSM_HEADS

LANES = 128
HALO = 8
VMEM_LIMIT = 56 * 1024 * 1024

COL_Z, COL_XS, COL_BC, COL_Q, COL_K, COL_V, COL_GS, COL_GA = (
    0, 2048, 4096, 5120, 6144, 7168, 8192, 9216)
PROJ_W = 10240
OUT_K, OUT_GS, OUT_GA = 5120, 6144, 7168
OUT_W = 8192


def _out_col(col):
    assert not (COL_Q <= col < COL_K or COL_V <= col < COL_GS)
    if col < COL_Q:
        return col
    return col - (COL_K - OUT_K) if col < COL_V else col - (COL_GS - OUT_GS)

LOG2E = 1.4426950408889634


def _silu(x):
    h = 0.5 * x
    return h + h * jnp.tanh(h)


def _cparams(sem):
    return pltpu.CompilerParams(dimension_semantics=sem, vmem_limit_bytes=VMEM_LIMIT)


def _ada_kernel(c_ref, w_ref, b_ref, o_ref):
    c = c_ref[...]
    ca = c * jax.nn.sigmoid(c)
    o_ref[0] = jnp.dot(ca, w_ref[0], precision=HIGHEST,
                       preferred_element_type=F32) + b_ref[0]


def _ada(c, w_ada, b_ada):
    depth, d, n = w_ada.shape
    bsz = c.shape[0]
    tn = 1024
    return pl.pallas_call(
        _ada_kernel,
        grid=(depth, n // tn),
        in_specs=[
            pl.BlockSpec((bsz, d), lambda l, j: (0, 0)),
            pl.BlockSpec((1, d, tn), lambda l, j: (l, 0, j)),
            pl.BlockSpec((1, 1, tn), lambda l, j: (l, 0, j)),
        ],
        out_specs=pl.BlockSpec((1, bsz, tn), lambda l, j: (l, 0, j)),
        out_shape=jax.ShapeDtypeStruct((depth, bsz, n), F32),
        compiler_params=_cparams(("arbitrary", "arbitrary")),
        name="ada",
    )(c, w_ada, b_ada.reshape(depth, 1, n))


def _rope_tile(acc, cos, sin, scale):
    lane = lax.broadcasted_iota(jnp.int32, cos.shape, 1)
    first_half = (lane % DIFF_HEAD_DIM) < (DIFF_HEAD_DIM // 2)
    outs = []
    for hh in range(acc.shape[1] // LANES):
        t = acc[:, hh * LANES:(hh + 1) * LANES]
        sw = jnp.where(first_half, pltpu.roll(t, LANES - 32, 1), pltpu.roll(t, 32, 1))
        outs.append((t * cos + sw * sin) * scale)
    return jnp.concatenate(outs, axis=1)


def _conv_silu_tile(pad_scr, hist_scr, cw_ref, cb_ref, ccs):
    tm = pad_scr.shape[0] - HALO
    nh = SSM_CONV - 1
    pad_scr[HALO - nh:HALO, :] = hist_scr[HALO - nh:HALO, ccs]
    y = cb_ref[:, ccs] + cw_ref[nh:nh + 1, ccs] * pad_scr[HALO:HALO + tm, :]
    for k in range(nh):
        off = HALO - nh + k
        y = y + cw_ref[k:k + 1, ccs] * pad_scr[off:off + tm, :]
    hist_scr[HALO - nh:HALO, ccs] = pad_scr[HALO + tm - nh:HALO + tm, :]
    return _silu(y)


def _inproj_kernel(x_ref, g_ref, sc_ref, sh_ref, w_ref, wdt_ref, cos_ref, sin_ref,
                   cw_ref, cb_ref, o_ref, dt_ref, qt_ref, vt_ref, pad_scr, hist_scr,
                   *, tn, nps, qscale):
    @pl.when(pl.program_id(0) % nps == 0)
    def _():
        hist_scr[...] = jnp.zeros_like(hist_scr)

    xf = x_ref[...]
    ms = jnp.mean(xf * xf, axis=-1, keepdims=True)
    h = xf * lax.rsqrt(ms + EPS) * g_ref[...]
    h = h * (1.0 + sc_ref[0]) + sh_ref[0]
    hb = h.astype(BF16)
    dt_ref[...] = jnp.dot(hb, wdt_ref[...], preferred_element_type=F32)
    cols = list(range(0, PROJ_W, tn))
    heavy = [c for c in cols if COL_XS <= c < COL_GS]
    light = [c for c in cols if not COL_XS <= c < COL_GS]
    order = []
    while heavy or light:
        if heavy:
            order.append(heavy.pop(0))
        if light:
            order.append(light.pop(0))
    tm = x_ref.shape[0]
    pending = None
    n_staged = 0
    for col in order + [None]:
        staged = None
        if col is not None:
            cs = slice(col, col + tn)
            acc = jnp.dot(hb, w_ref[:, cs], preferred_element_type=F32)
            if COL_XS <= col < COL_GS:
                slot = n_staged % pad_scr.shape[0]
                n_staged += 1
                pad_scr[slot, HALO:HALO + tm, :] = acc
                staged = (slot, col)
            else:
                ocol = _out_col(col)
                o_ref[:, ocol:ocol + tn] = acc.astype(o_ref.dtype)
        if pending is not None:
            slot, pcol = pending
            if pcol < COL_Q:
                y = _conv_silu_tile(pad_scr.at[slot], hist_scr, cw_ref, cb_ref,
                                    slice(pcol - COL_XS, pcol - COL_XS + tn))
            elif pcol < COL_V:
                y = _rope_tile(pad_scr[slot, HALO:HALO + tm, :], cos_ref[...], sin_ref[...],
                               qscale if pcol < COL_K else 1.0)
            else:
                y = pad_scr[slot, HALO:HALO + tm, :]
            if COL_Q <= pcol < COL_K:
                qt_ref[0, 0, pcol - COL_Q:pcol - COL_Q + tn, :] = y.astype(BF16).T
            elif COL_V <= pcol:
                vt_ref[0, 0, pcol - COL_V:pcol - COL_V + tn, :] = y.astype(BF16).T
            else:
                ocol = _out_col(pcol)
                o_ref[:, ocol:ocol + tn] = y.astype(o_ref.dtype)
        pending = staged


def _inproj(x2, g, sc, sh, w_main, w_dt, cos_t, sin_t, conv_w, conv_b, seq):
    t, d = x2.shape
    tm, tn = 512, 256
    nps = seq // tm
    conv_dim = conv_w.shape[1]
    qscale = (DIFF_HEAD_DIM ** -0.5) * LOG2E
    kern = functools.partial(_inproj_kernel, tn=tn, nps=nps, qscale=qscale)
    resident = dict(pipeline_mode=pl.Buffered(1))
    return pl.pallas_call(
        kern,
        grid=(t // tm,),
        in_specs=[
            pl.BlockSpec((tm, d), lambda i: (i, 0)),
            pl.BlockSpec((1, d), lambda i: (0, 0)),
            pl.BlockSpec((1, 1, d), lambda i: (i // nps, 0, 0)),
            pl.BlockSpec((1, 1, d), lambda i: (i // nps, 0, 0)),
            pl.BlockSpec((d, PROJ_W), lambda i: (0, 0), **resident),
            pl.BlockSpec((d, LANES), lambda i: (0, 0), **resident),
            pl.BlockSpec((tm, LANES), lambda i: (i % nps, 0)),
            pl.BlockSpec((tm, LANES), lambda i: (i % nps, 0)),
            pl.BlockSpec((SSM_CONV, conv_dim), lambda i: (0, 0)),
            pl.BlockSpec((1, conv_dim), lambda i: (0, 0)),
        ],
        out_specs=[
            pl.BlockSpec((tm, OUT_W), lambda i: (i, 0)),
            pl.BlockSpec((tm, LANES), lambda i: (i, 0)),
            pl.BlockSpec((1, 1, DIFF_HEADS * LANES, tm), lambda i: (i // nps, i % nps, 0, 0)),
            pl.BlockSpec((1, 1, DIFF_HEADS * LANES, tm), lambda i: (i // nps, i % nps, 0, 0)),
        ],
        out_shape=[
            jax.ShapeDtypeStruct((t, OUT_W), BF16),
            jax.ShapeDtypeStruct((t, LANES), F32),
            jax.ShapeDtypeStruct((t // seq, nps, DIFF_HEADS * LANES, tm), BF16),
            jax.ShapeDtypeStruct((t // seq, nps, DIFF_HEADS * LANES, tm), BF16),
        ],
        scratch_shapes=[
            pltpu.VMEM((4, HALO + tm, tn), F32),
            pltpu.VMEM((HALO, conv_dim), F32),
        ],
        compiler_params=_cparams(("arbitrary",)),
        name="inproj",
    )(x2, g, sc, sh, w_main, w_dt, cos_t, sin_t, conv_w, conv_b)


def _softplus(x):
    e = jnp.exp(-jnp.abs(x))
    u = 1.0 + e
    log1p_e = jnp.where(u == 1.0, e, jnp.log(u) * (e / (u - 1.0)))
    return jnp.maximum(x, 0.0) + log1p_e


def _split3(x):
    p0 = x.astype(BF16)
    r1 = x - p0.astype(F32)
    p1 = r1.astype(BF16)
    p2 = (r1 - p1.astype(F32)).astype(BF16)
    return p0, p1, p2


def _cumsum_rows(tri, x):
    return sum(jnp.dot(tri, p, preferred_element_type=F32) for p in _split3(x))


def _cumsum_cols(x, tri):
    return sum(jnp.dot(p, tri, preferred_element_type=F32) for p in _split3(x))


def _ssd_kernel(z_ref, xs_ref, bc_ref, dt_ref, dtb_ref, alog_ref, dexp_ref, ng_ref, o_ref,
                state_scr, y_scr):
    @pl.when(pl.program_id(1) == 0)
    def _():
        state_scr[...] = jnp.zeros_like(state_scr)

    for ci in range(z_ref.shape[0] // SSM_CHUNK):
        rows = pl.ds(ci * SSM_CHUNK, SSM_CHUNK)
        _ssd_chunk(z_ref.at[rows], xs_ref.at[rows], bc_ref.at[rows], dt_ref.at[rows],
                   dtb_ref, alog_ref, dexp_ref, ng_ref, o_ref.at[rows], state_scr,
                   y_scr.at[rows])


def _ssd_chunk(z_ref, xs_ref, bc_ref, dt_ref, dtb_ref, alog_ref, dexp_ref, ng_ref, o_ref,
               state_scr, y_scr):
    lc = SSM_CHUNK
    n = SSM_STATE
    xs_b = xs_ref[...]
    bc = bc_ref[...].astype(F32)

    dtp = _softplus(dt_ref[...] + dtb_ref[...])
    a_step = dtp * (-LOG2E * jnp.exp(alog_ref[...]))
    row_i = lax.broadcasted_iota(jnp.int32, (lc, lc), 0)
    col_i = lax.broadcasted_iota(jnp.int32, (lc, lc), 1)
    tril = row_i >= col_i
    tril_b = jnp.where(tril, 1.0, 0.0).astype(BF16)
    triu_b = jnp.where(row_i <= col_i, 1.0, 0.0).astype(BF16)
    acum = _cumsum_rows(tril_b, a_step)
    nh8 = SSM_HEADS
    a_step_t = a_step.T[:nh8, :]
    dt_t = dtp.T[:nh8, :]
    acum_t = _cumsum_cols(a_step_t, triu_b)
    alast_t = acum_t[:, lc - 1:lc]
    w2_t = dt_t * jnp.exp2(alast_t - acum_t)
    dlast = jnp.broadcast_to(jnp.exp2(alast_t), (nh8, LANES))
    arow_t = acum_t - jnp.log2(dt_t)

    lane = lax.broadcasted_iota(jnp.int32, (1, LANES), 1)
    lo = lane < SSM_HEAD_DIM
    heads_per_group = SSM_HEADS // SSM_GROUPS
    for g in range(SSM_GROUPS):
        bm = bc[:, g * n:(g + 1) * n]
        cm = bc[:, SSM_GROUPS * n + g * n:SSM_GROUPS * n + (g + 1) * n]
        bm_t = bm.T
        cb = jnp.dot(cm.astype(BF16), bm_t.astype(BF16), preferred_element_type=F32)
        for pp in range(heads_per_group // 2):
            p = g * (heads_per_group // 2) + pp
            cs = slice(p * LANES, (p + 1) * LANES)
            x_pair = xs_b[:, cs]
            s_pair = state_scr[:, cs]
            s_pair_b = s_pair.astype(BF16)
            y = jnp.zeros((lc, LANES), F32)
            s_new = jnp.zeros((n, LANES), F32)
            for e in range(2):
                h = 2 * p + e
                sel = lo if e == 0 else jnp.logical_not(lo)
                x_h = jnp.where(sel, x_pair, jnp.zeros_like(x_pair))
                s_h = jnp.where(sel, s_pair_b, jnp.zeros_like(s_pair_b))
                acol = jnp.broadcast_to(acum[:, h:h + 1], (lc, lc))
                seg = acol - arow_t[h:h + 1, :]
                decay_dt = jnp.exp2(jnp.where(tril, seg, -1e30))
                m_h = (decay_dt * cb).astype(BF16)
                y = y + jnp.dot(m_h, x_h, preferred_element_type=F32)
                c_h = (cm * jnp.exp2(acol)).astype(BF16)
                y = y + jnp.dot(c_h, s_h, preferred_element_type=F32)
                b_h = (bm_t * w2_t[h:h + 1, :]).astype(BF16)
                s_new = s_new + jnp.dot(b_h, x_h, preferred_element_type=F32)
            d_row = jnp.where(lo, dlast[2 * p:2 * p + 1, :], dlast[2 * p + 1:2 * p + 2, :])
            state_scr[:, cs] = s_pair * d_row + s_new
            y_scr[:, cs] = y + dexp_ref[:, cs] * x_pair.astype(F32)

    zf = z_ref[...].astype(F32)
    yg = y_scr[...] * _silu(zf)
    gw = SSM_D_INNER // SSM_GROUPS
    outs = []
    for g in range(SSM_GROUPS):
        blk = yg[:, g * gw:(g + 1) * gw]
        ms = jnp.mean(blk * blk, axis=-1, keepdims=True)
        outs.append(blk * lax.rsqrt(ms + EPS))
    o_ref[...] = (jnp.concatenate(outs, axis=1) * ng_ref[...]).astype(o_ref.dtype)


def _ssd(proj, dt, dtb, alog, dexp, ng, bsz, seq):
    t = proj.shape[0]
    lc = 4 * SSM_CHUNK
    nc = seq // lc
    xw, bw = SSM_D_INNER, 2 * SSM_GROUPS * SSM_STATE
    row = lambda b, c: b * nc + c
    full = lambda b, c: (0, 0)
    return pl.pallas_call(
        _ssd_kernel,
        grid=(bsz, nc),
        in_specs=[
            pl.BlockSpec((lc, xw), lambda b, c: (row(b, c), COL_Z // xw)),
            pl.BlockSpec((lc, xw), lambda b, c: (row(b, c), COL_XS // xw)),
            pl.BlockSpec((lc, bw), lambda b, c: (row(b, c), COL_BC // bw)),
            pl.BlockSpec((lc, LANES), lambda b, c: (row(b, c), 0)),
            pl.BlockSpec((1, LANES), full),
            pl.BlockSpec((1, LANES), full),
            pl.BlockSpec((1, xw), full),
            pl.BlockSpec((1, xw), full),
        ],
        out_specs=pl.BlockSpec((lc, xw), lambda b, c: (row(b, c), 0)),
        out_shape=jax.ShapeDtypeStruct((t, xw), BF16),
        scratch_shapes=[
            pltpu.VMEM((SSM_STATE, xw), F32),
            pltpu.VMEM((lc, xw), F32),
        ],
        compiler_params=_cparams(("arbitrary", "arbitrary")),
        name="ssd",
    )(proj, proj, proj, dt, dtb, alog, dexp, ng)


def _attn_head_kernel(qt_ref, k_ref, vt_ref, lam_ref, sg_ref, o_ref,
                      qt_scr, m_scr, l_scr, acc_scr, sa_scr, sb_scr,
                      *, tk, nsub, cw, lam_init):
    tq = nsub * tk
    w = 2 * tq
    assert nsub % 2 == 0 and tk % cw == 0
    nq = qt_scr.shape[0]

    def prep_q(u, carry):
        qt = jnp.concatenate([qt_ref[0, nsub * u + i] for i in range(nsub)], axis=1)
        comp = lax.broadcasted_iota(jnp.int32, qt.shape, 0) < DIFF_HEAD_DIM
        zero = jnp.zeros_like(qt)
        qt_scr[u] = jnp.concatenate([jnp.where(comp, qt, zero), jnp.where(comp, zero, qt)], axis=1)
        return carry
    lax.fori_loop(0, nq, prep_q, 0)

    def reset_state():
        m_scr[...] = jnp.full(m_scr.shape, -1e30, F32)
        l_scr[...] = jnp.zeros(l_scr.shape, F32)
        acc_scr[...] = jnp.zeros(acc_scr.shape, F32)

    def k_block(j):
        return k_ref[pl.ds(pl.multiple_of(j * tk, tk), tk), :]

    def scores(u, j, s_ref):
        s_ref[...] = jnp.dot(k_block(j), qt_scr[u], preferred_element_type=F32)

    def chunk_row(c):
        return (c * cw) % tq

    def score_chunks(u, j, s_ref, min_row):
        def one(c):
            cs = slice(c * cw, (c + 1) * cw)
            def run():
                s_ref[:, cs] = jnp.dot(k_block(j), qt_scr[u, :, cs], preferred_element_type=F32)
            return run
        return [one(c) for c in range(w // cw) if chunk_row(c) >= min_row]

    def interleave(first, second):
        for i in range(max(len(first), len(second))):
            if i < len(first):
                first[i]()
            if i < len(second):
                second[i]()

    def update_chunk(s_ref, j, c, nkeys, row_off):
        cs = slice(c * cw, (c + 1) * cw)
        s = s_ref[0:nkeys, cs]
        if row_off is not None:
            key = lax.broadcasted_iota(jnp.int32, (nkeys, cw), 0)
            row = lax.broadcasted_iota(jnp.int32, (nkeys, cw), 1) + row_off
            s = jnp.where(key <= row, s, -1e30)
        m_old = m_scr[:, cs]
        m_new = jnp.maximum(m_old, jnp.max(s, axis=0, keepdims=True))
        alpha = jnp.exp2(m_old - m_new)
        p = jnp.exp2(s - m_new)
        l_scr[:, cs] = alpha * l_scr[:, cs] + jnp.sum(p, axis=0, keepdims=True)
        acc_scr[:, cs] = alpha * acc_scr[:, cs] + jnp.dot(
            vt_ref[0, j, :, 0:nkeys], p.astype(BF16), preferred_element_type=F32)
        m_scr[:, cs] = m_new

    def update_full(j, s_ref):
        return [functools.partial(update_chunk, s_ref, j, c, tk, None) for c in range(w // cw)]

    def update_diag(j, s_ref, i):
        out = []
        for c in range(w // cw):
            row_off = chunk_row(c) - i * tk
            if 0 <= row_off < tk:
                out.append(functools.partial(update_chunk, s_ref, j, c, row_off + cw, row_off))
            elif row_off >= tk:
                out.append(functools.partial(update_chunk, s_ref, j, c, tk, None))
        return out

    def finalize(u):
        lam = (jnp.exp(jnp.sum(lam_ref[0:1, :] * lam_ref[1:2, :], axis=-1, keepdims=True))
               - jnp.exp(jnp.sum(lam_ref[2:3, :] * lam_ref[3:4, :], axis=-1, keepdims=True))
               + lam_init)
        ot = acc_scr[...] / l_scr[...]
        o = (ot[:, :tq] - lam * ot[:, tq:]).T
        ms = jnp.mean(o * o, axis=-1, keepdims=True)
        o = o * lax.rsqrt(ms + EPS) * sg_ref[...] * (1.0 - lam_init)
        o_ref[pl.ds(pl.multiple_of(u * tq, tq), tq), :] = o.astype(o_ref.dtype)

    reset_state()
    scores(0, 0, sa_scr)

    def tile(u, carry):
        def pair(j):
            interleave(score_chunks(u, j + 1, sb_scr, 0), update_full(j, sa_scr))
            interleave(score_chunks(u, j + 2, sa_scr, 0), update_full(j + 1, sb_scr))

        npairs = (nsub // 2) * u

        def two_pairs(t2, c2):
            pair(4 * t2)
            pair(4 * t2 + 2)
            return c2
        lax.fori_loop(0, npairs // 2, two_pairs, 0)

        @pl.when(npairs % 2 == 1)
        def _():
            pair(2 * (npairs - 1))

        jd = nsub * u
        u_next = jnp.minimum(u + 1, nq - 1)
        for i in range(nsub):
            cur, nxt = (sa_scr, sb_scr) if i % 2 == 0 else (sb_scr, sa_scr)
            if i + 1 < nsub:
                ahead = score_chunks(u, jd + i + 1, nxt, (i + 1) * tk)
            else:
                ahead = score_chunks(u_next, 0, nxt, 0)
            interleave(ahead, update_diag(jd + i, cur, i))
        finalize(u)
        reset_state()
        return carry

    lax.fori_loop(0, nq, tile, 0)


def _attn_head(proj, qt, vt, lam4, sg, bsz, seq, lam_init):
    t = proj.shape[0]
    nkv, tk = qt.shape[1], qt.shape[3]
    nsub = 2
    tq = nsub * tk
    cw = 256
    kern = functools.partial(_attn_head_kernel, tk=tk, nsub=nsub, cw=cw, lam_init=lam_init)
    return pl.pallas_call(
        kern,
        grid=(bsz, DIFF_HEADS),
        in_specs=[
            pl.BlockSpec((1, nkv, LANES, tk), lambda b, h: (b, 0, h, 0)),
            pl.BlockSpec((seq, LANES), lambda b, h: (b, OUT_K // LANES + h)),
            pl.BlockSpec((1, nkv, LANES, tk), lambda b, h: (b, 0, h, 0)),
            pl.BlockSpec((4, DIFF_HEAD_DIM), lambda b, h: (0, 0)),
            pl.BlockSpec((1, LANES), lambda b, h: (0, 0)),
        ],
        out_specs=pl.BlockSpec((seq, LANES), lambda b, h: (b, h)),
        out_shape=jax.ShapeDtypeStruct((t, DIFF_HEADS * LANES), BF16),
        scratch_shapes=[
            pltpu.VMEM((seq // tq, LANES, 2 * tq), BF16),
            pltpu.VMEM((1, 2 * tq), F32),
            pltpu.VMEM((1, 2 * tq), F32),
            pltpu.VMEM((LANES, 2 * tq), F32),
            pltpu.VMEM((tk, 2 * tq), F32),
            pltpu.VMEM((tk, 2 * tq), F32),
        ],
        compiler_params=_cparams(("arbitrary", "arbitrary")),
        name="attn",
    )(qt, proj, vt, lam4, sg)


def _merge_kernel(x_ref, ys_ref, ya_ref, gs_ref, ga_ref, g1_ref, ws_ref, wa_ref, wo_ref, o_ref):
    ys = jnp.dot(ys_ref[...], ws_ref[...], preferred_element_type=F32)
    ya = jnp.dot(ya_ref[...], wa_ref[...], preferred_element_type=F32)
    mixed = (jax.nn.sigmoid(gs_ref[...].astype(F32)) * ys
             + jax.nn.sigmoid(ga_ref[...].astype(F32)) * ya)
    upd = jnp.dot(mixed.astype(BF16), wo_ref[...], preferred_element_type=F32)
    o_ref[...] = x_ref[...] + g1_ref[0] * upd


def _merge(x2, yssm, yatt, proj, g1, ws, wa, wo, seq):
    t, d = x2.shape
    tm = 512
    nps = seq // tm
    const = lambda i: (0, 0)
    return pl.pallas_call(
        _merge_kernel,
        grid=(t // tm,),
        in_specs=[
            pl.BlockSpec((tm, d), lambda i: (i, 0)),
            pl.BlockSpec((tm, SSM_D_INNER), lambda i: (i, 0)),
            pl.BlockSpec((tm, d), lambda i: (i, 0)),
            pl.BlockSpec((tm, d), lambda i: (i, OUT_GS // d)),
            pl.BlockSpec((tm, d), lambda i: (i, OUT_GA // d)),
            pl.BlockSpec((1, 1, d), lambda i: (i // nps, 0, 0)),
            pl.BlockSpec((SSM_D_INNER, d), const),
            pl.BlockSpec((d, d), const),
            pl.BlockSpec((d, d), const),
        ],
        out_specs=pl.BlockSpec((tm, d), lambda i: (i, 0)),
        out_shape=jax.ShapeDtypeStruct((t, d), F32),
        compiler_params=_cparams(("arbitrary",)),
        name="merge",
    )(x2, yssm, yatt, proj, proj, g1, ws, wa, wo)


def _mlp_kernel(x_ref, g_ref, sc_ref, sh_ref, g2_ref, w1_ref, w2_ref, fg_ref, o_ref, *, final):
    xf = x_ref[...]
    ms = jnp.mean(xf * xf, axis=-1, keepdims=True)
    h = xf * lax.rsqrt(ms + EPS) * g_ref[...]
    h = h * (1.0 + sc_ref[0]) + sh_ref[0]
    u = jnp.dot(h.astype(BF16), w1_ref[...], preferred_element_type=F32)
    u = jnp.square(jnp.maximum(u, 0.0))
    y = xf + g2_ref[0] * jnp.dot(u.astype(BF16), w2_ref[...], preferred_element_type=F32)
    if final:
        ms2 = jnp.mean(y * y, axis=-1, keepdims=True)
        y = y * lax.rsqrt(ms2 + EPS) * fg_ref[...]
    o_ref[...] = y


def _mlp(x2, g, sc, sh, g2, w1, w2, fg, seq, final):
    t, d = x2.shape
    tm = 512
    nps = seq // tm
    const = lambda i: (0, 0)
    per_b = lambda i: (i // nps, 0, 0)
    return pl.pallas_call(
        functools.partial(_mlp_kernel, final=final),
        grid=(t // tm,),
        in_specs=[
            pl.BlockSpec((tm, d), lambda i: (i, 0)),
            pl.BlockSpec((1, d), const),
            pl.BlockSpec((1, 1, d), per_b),
            pl.BlockSpec((1, 1, d), per_b),
            pl.BlockSpec((1, 1, d), per_b),
            pl.BlockSpec((d, D_FF), const),
            pl.BlockSpec((D_FF, d), const),
            pl.BlockSpec((1, d), const),
        ],
        out_specs=pl.BlockSpec((tm, d), lambda i: (i, 0)),
        out_shape=jax.ShapeDtypeStruct((t, d), F32),
        compiler_params=_cparams(("arbitrary",)),
        name="mlp",
    )(x2, g, sc, sh, g2, w1, w2, fg)


def _rope_tables(seq):
    half = DIFF_HEAD_DIM // 2
    inv = 1.0 / (ROPE_THETA ** (jnp.arange(0, DIFF_HEAD_DIM, 2, dtype=F32) / DIFF_HEAD_DIM))
    ang = jnp.arange(seq, dtype=F32)[:, None] * inv[None, :]
    cos, sin = jnp.cos(ang), jnp.sin(ang)
    reps = LANES // half
    cos_t = jnp.tile(cos, (1, reps))
    sin_t = jnp.tile(jnp.concatenate([-sin, sin], axis=1), (1, reps // 2))
    return cos_t, sin_t


def _lambda_init(layer_idx):
    return 0.8 - 0.6 * math.exp(-0.3 * layer_idx)


def kernel(x, c, w_ada, b_ada, norm1_g, w_in, conv_w, conv_b, dt_bias, a_log, d_skip,
           ssm_norm_g, lam_q1, lam_k1, lam_q2, lam_k2, subln_g, w_ssm_out, w_attn_out,
           w_out, norm2_g, w_ff1, w_ff2, final_g):
    bsz, seq, d = x.shape
    depth = w_in.shape[0]
    assert d == D_MODEL and seq % 1024 == 0
    t = bsz * seq
    cos_t, sin_t = _rope_tables(seq)
    ada = _ada(c, w_ada, b_ada)
    x2 = x.reshape(t, d)
    pad_h = LANES - SSM_HEADS
    for l in range(depth):
        mods = [ada[l, :, i * d:(i + 1) * d].reshape(bsz, 1, d) for i in range(N_ADA)]
        sh1, sc1, g1, sh2, sc2, g2 = mods
        w_l = w_in[l]
        w_main = jnp.concatenate([w_l[:, :DT_OFF], w_l[:, DT_OFF + DT_W:]], axis=1).astype(BF16)
        w_dt = jnp.pad(w_l[:, DT_OFF:DT_OFF + DT_W], ((0, 0), (0, pad_h))).astype(BF16)
        proj, dt, qt, vt = _inproj(x2, norm1_g[l].reshape(1, d), sc1, sh1, w_main, w_dt,
                           cos_t, sin_t, conv_w[l], conv_b[l].reshape(1, -1), seq)
        yssm = _ssd(
            proj, dt,
            jnp.pad(dt_bias[l], (0, pad_h)).reshape(1, LANES),
            jnp.pad(a_log[l], (0, pad_h)).reshape(1, LANES),
            jnp.repeat(d_skip[l], SSM_HEAD_DIM).reshape(1, SSM_D_INNER),
            ssm_norm_g[l].reshape(1, SSM_D_INNER), bsz, seq)
        lam4 = jnp.stack([lam_q1[l], lam_k1[l], lam_q2[l], lam_k2[l]], axis=0)
        yatt = _attn_head(proj, qt, vt, lam4, subln_g[l].reshape(1, LANES), bsz, seq, _lambda_init(l))
        x2 = _merge(x2, yssm, yatt, proj, g1, w_ssm_out[l].astype(BF16),
                    w_attn_out[l].astype(BF16), w_out[l].astype(BF16), seq)
        x2 = _mlp(x2, norm2_g[l].reshape(1, d), sc2, sh2, g2, w_ff1[l].astype(BF16),
                  w_ff2[l].astype(BF16), final_g.reshape(1, d), seq, l == depth - 1)
    return x2.reshape(bsz, seq, d)
```

```python
import functools
import math

import jax
import jax.numpy as jnp
from jax import lax
from jax.experimental import pallas as pl
from jax.experimental.pallas import tpu as pltpu

F32 = jnp.float32
BF16 = jnp.bfloat16
HIGHEST = lax.Precision.HIGHEST

D_MODEL = 1024
SSM_D_INNER = 2048
SSM_HEAD_DIM = 64
SSM_HEADS = 32
SSM_GROUPS = 4
SSM_STATE = 128
SSM_CONV = 4
SSM_CHUNK = 128
DIFF_HEADS = 8
DIFF_HEAD_DIM = 64
D_FF = 4096
N_ADA = 6
EPS = 1e-5
ROPE_THETA = 10000.0
DT_OFF = 5120
DT_W = SSM_HEADS
LOG2E = 1.4426950408889634

LANES = 128
HALO = 8
VMEM_LIMIT = 56 * 1024 * 1024

COL_Z, COL_XS, COL_BC, COL_Q, COL_K, COL_V, COL_GS, COL_GA = (
    0, 2048, 4096, 5120, 6144, 7168, 8192, 9216)
PROJ_W = 10240
OUT_K, OUT_GS, OUT_GA = 5120, 6144, 7168
OUT_W = 8192


def _out_col(col):
    assert not (COL_Q <= col < COL_K or COL_V <= col < COL_GS)
    if col < COL_Q:
        return col
    return col - (COL_K - OUT_K) if col < COL_V else col - (COL_GS - OUT_GS)


def _silu(x):
    h = 0.5 * x
    return h + h * jnp.tanh(h)


def _cparams(sem):
    return pltpu.CompilerParams(dimension_semantics=sem, vmem_limit_bytes=VMEM_LIMIT)


def _ada_kernel(c_ref, w_ref, b_ref, o_ref):
    c = c_ref[...]
    ca = c * jax.nn.sigmoid(c)
    o_ref[0] = jnp.dot(ca, w_ref[0], precision=HIGHEST,
                       preferred_element_type=F32) + b_ref[0]


def _ada(c, w_ada, b_ada):
    depth, d, n = w_ada.shape
    bsz = c.shape[0]
    tn = 1024
    return pl.pallas_call(
        _ada_kernel,
        grid=(depth, n // tn),
        in_specs=[
            pl.BlockSpec((bsz, d), lambda l, j: (0, 0)),
            pl.BlockSpec((1, d, tn), lambda l, j: (l, 0, j)),
            pl.BlockSpec((1, 1, tn), lambda l, j: (l, 0, j)),
        ],
        out_specs=pl.BlockSpec((1, bsz, tn), lambda l, j: (l, 0, j)),
        out_shape=jax.ShapeDtypeStruct((depth, bsz, n), F32),
        compiler_params=_cparams(("arbitrary", "arbitrary")),
        name="ada",
    )(c, w_ada, b_ada.reshape(depth, 1, n))


def _rope_tile(acc, cos, sin, scale):
    half = DIFF_HEAD_DIM // 2
    lane = lax.broadcasted_iota(jnp.int32, cos.shape, 1)
    first_half = (lane % DIFF_HEAD_DIM) < half
    outs = []
    for hh in range(acc.shape[1] // LANES):
        t = acc[:, hh * LANES:(hh + 1) * LANES]
        sw = jnp.where(first_half, pltpu.roll(t, LANES - half, 1), pltpu.roll(t, half, 1))
        outs.append((t * cos + sw * sin) * scale)
    return jnp.concatenate(outs, axis=1)


def _conv_silu_tile(pad_scr, hist_scr, cw_ref, cb_ref, ccs):
    tm = pad_scr.shape[0] - HALO
    nh = SSM_CONV - 1
    pad_scr[HALO - nh:HALO, :] = hist_scr[HALO - nh:HALO, ccs]
    y = cb_ref[:, ccs] + cw_ref[nh:nh + 1, ccs] * pad_scr[HALO:HALO + tm, :]
    for k in range(nh):
        off = HALO - nh + k
        y = y + cw_ref[k:k + 1, ccs] * pad_scr[off:off + tm, :]
    hist_scr[HALO - nh:HALO, ccs] = pad_scr[HALO + tm - nh:HALO + tm, :]
    return _silu(y)


def _inproj_kernel(x_ref, g_ref, sc_ref, sh_ref, w_ref, wdt_ref, cos_ref, sin_ref,
                   cw_ref, cb_ref, o_ref, dt_ref, qt_ref, vt_ref, pad_scr, hist_scr,
                   *, tn, nps, qscale):
    @pl.when(pl.program_id(0) % nps == 0)
    def _():
        hist_scr[...] = jnp.zeros_like(hist_scr)

    xf = x_ref[...]
    ms = jnp.mean(xf * xf, axis=-1, keepdims=True)
    h = xf * lax.rsqrt(ms + EPS) * g_ref[...]
    h = h * (1.0 + sc_ref[0]) + sh_ref[0]
    hb = h.astype(BF16)
    dt_ref[...] = jnp.dot(hb, wdt_ref[...], preferred_element_type=F32)
    cols = list(range(0, PROJ_W, tn))
    heavy = [c for c in cols if COL_XS <= c < COL_GS]
    light = [c for c in cols if not COL_XS <= c < COL_GS]
    order = []
    while heavy or light:
        if heavy:
            order.append(heavy.pop(0))
        if light:
            order.append(light.pop(0))
    tm = x_ref.shape[0]
    pending = None
    n_staged = 0
    for col in order + [None]:
        staged = None
        if col is not None:
            cs = slice(col, col + tn)
            acc = jnp.dot(hb, w_ref[:, cs], preferred_element_type=F32)
            if COL_XS <= col < COL_GS:
                slot = n_staged % pad_scr.shape[0]
                n_staged += 1
                pad_scr[slot, HALO:HALO + tm, :] = acc
                staged = (slot, col)
            else:
                ocol = _out_col(col)
                o_ref[:, ocol:ocol + tn] = acc.astype(o_ref.dtype)
        if pending is not None:
            slot, pcol = pending
            if pcol < COL_Q:
                y = _conv_silu_tile(pad_scr.at[slot], hist_scr, cw_ref, cb_ref,
                                    slice(pcol - COL_XS, pcol - COL_XS + tn))
            elif pcol < COL_V:
                y = _rope_tile(pad_scr[slot, HALO:HALO + tm, :], cos_ref[...], sin_ref[...],
                               qscale if pcol < COL_K else 1.0)
            else:
                y = pad_scr[slot, HALO:HALO + tm, :]
            if COL_Q <= pcol < COL_K:
                qt_ref[0, 0, pcol - COL_Q:pcol - COL_Q + tn, :] = y.astype(BF16).T
            elif COL_V <= pcol:
                vt_ref[0, 0, pcol - COL_V:pcol - COL_V + tn, :] = y.astype(BF16).T
            else:
                ocol = _out_col(pcol)
                o_ref[:, ocol:ocol + tn] = y.astype(o_ref.dtype)
        pending = staged


def _inproj(x2, g, sc, sh, w_main, w_dt, cos_t, sin_t, conv_w, conv_b, seq):
    t, d = x2.shape
    tm, tn = 512, 256
    nps = seq // tm
    conv_dim = conv_w.shape[1]
    qscale = (DIFF_HEAD_DIM ** -0.5) * LOG2E
    kern = functools.partial(_inproj_kernel, tn=tn, nps=nps, qscale=qscale)
    resident = dict(pipeline_mode=pl.Buffered(1))
    return pl.pallas_call(
        kern,
        grid=(t // tm,),
        in_specs=[
            pl.BlockSpec((tm, d), lambda i: (i, 0)),
            pl.BlockSpec((1, d), lambda i: (0, 0)),
            pl.BlockSpec((1, 1, d), lambda i: (i // nps, 0, 0)),
            pl.BlockSpec((1, 1, d), lambda i: (i // nps, 0, 0)),
            pl.BlockSpec((d, PROJ_W), lambda i: (0, 0), **resident),
            pl.BlockSpec((d, LANES), lambda i: (0, 0), **resident),
            pl.BlockSpec((tm, LANES), lambda i: (i % nps, 0)),
            pl.BlockSpec((tm, LANES), lambda i: (i % nps, 0)),
            pl.BlockSpec((SSM_CONV, conv_dim), lambda i: (0, 0)),
            pl.BlockSpec((1, conv_dim), lambda i: (0, 0)),
        ],
        out_specs=[
            pl.BlockSpec((tm, OUT_W), lambda i: (i, 0)),
            pl.BlockSpec((tm, LANES), lambda i: (i, 0)),
            pl.BlockSpec((1, 1, DIFF_HEADS * LANES, tm), lambda i: (i // nps, i % nps, 0, 0)),
            pl.BlockSpec((1, 1, DIFF_HEADS * LANES, tm), lambda i: (i // nps, i % nps, 0, 0)),
        ],
        out_shape=[
            jax.ShapeDtypeStruct((t, OUT_W), BF16),
            jax.ShapeDtypeStruct((t, LANES), F32),
            jax.ShapeDtypeStruct((t // seq, nps, DIFF_HEADS * LANES, tm), BF16),
            jax.ShapeDtypeStruct((t // seq, nps, DIFF_HEADS * LANES, tm), BF16),
        ],
        scratch_shapes=[
            pltpu.VMEM((4, HALO + tm, tn), F32),
            pltpu.VMEM((HALO, conv_dim), F32),
        ],
        compiler_params=_cparams(("arbitrary",)),
        name="inproj",
    )(x2, g, sc, sh, w_main, w_dt, cos_t, sin_t, conv_w, conv_b)


def _softplus(x):
    e = jnp.exp(-jnp.abs(x))
    u = 1.0 + e
    log1p_e = jnp.where(u == 1.0, e, jnp.log(u) * (e / (u - 1.0)))
    return jnp.maximum(x, 0.0) + log1p_e


def _split3(x):
    p0 = x.astype(BF16)
    r1 = x - p0.astype(F32)
    p1 = r1.astype(BF16)
    p2 = (r1 - p1.astype(F32)).astype(BF16)
    return p0, p1, p2


def _cumsum_rows(tri, x):
    return sum(jnp.dot(tri, p, preferred_element_type=F32) for p in _split3(x))


def _cumsum_cols(x, tri):
    return sum(jnp.dot(p, tri, preferred_element_type=F32) for p in _split3(x))


def _ssd_kernel(z_ref, xs_ref, bc_ref, dt_ref, dtb_ref, alog_ref, dexp_ref, ng_ref, o_ref,
                state_scr, y_scr):
    @pl.when(pl.program_id(1) == 0)
    def _():
        state_scr[...] = jnp.zeros_like(state_scr)

    for ci in range(z_ref.shape[0] // SSM_CHUNK):
        rows = pl.ds(ci * SSM_CHUNK, SSM_CHUNK)
        _ssd_chunk(z_ref.at[rows], xs_ref.at[rows], bc_ref.at[rows], dt_ref.at[rows],
                   dtb_ref, alog_ref, dexp_ref, ng_ref, o_ref.at[rows], state_scr,
                   y_scr.at[rows])


def _ssd_chunk(z_ref, xs_ref, bc_ref, dt_ref, dtb_ref, alog_ref, dexp_ref, ng_ref, o_ref,
               state_scr, y_scr):
    lc = SSM_CHUNK
    n = SSM_STATE
    xs_b = xs_ref[...]
    bc = bc_ref[...].astype(F32)

    dtp = _softplus(dt_ref[...] + dtb_ref[...])
    a_step = dtp * (-LOG2E * jnp.exp(alog_ref[...]))
    row_i = lax.broadcasted_iota(jnp.int32, (lc, lc), 0)
    col_i = lax.broadcasted_iota(jnp.int32, (lc, lc), 1)
    tril = row_i >= col_i
    tril_b = jnp.where(tril, 1.0, 0.0).astype(BF16)
    triu_b = jnp.where(row_i <= col_i, 1.0, 0.0).astype(BF16)
    acum = _cumsum_rows(tril_b, a_step)
    nh8 = SSM_HEADS
    a_step_t = a_step.T[:nh8, :]
    dt_t = dtp.T[:nh8, :]
    acum_t = _cumsum_cols(a_step_t, triu_b)
    alast_t = acum_t[:, lc - 1:lc]
    w2_t = dt_t * jnp.exp2(alast_t - acum_t)
    dlast = jnp.broadcast_to(jnp.exp2(alast_t), (nh8, LANES))
    arow_t = acum_t - jnp.log2(dt_t)

    lane = lax.broadcasted_iota(jnp.int32, (1, LANES), 1)
    lo = lane < SSM_HEAD_DIM
    heads_per_group = SSM_HEADS // SSM_GROUPS
    for g in range(SSM_GROUPS):
        bm = bc[:, g * n:(g + 1) * n]
        cm = bc[:, SSM_GROUPS * n + g * n:SSM_GROUPS * n + (g + 1) * n]
        bm_t = bm.T
        cb = jnp.dot(cm.astype(BF16), bm_t.astype(BF16), preferred_element_type=F32)
        for pp in range(heads_per_group // 2):
            p = g * (heads_per_group // 2) + pp
            cs = slice(p * LANES, (p + 1) * LANES)
            x_pair = xs_b[:, cs]
            s_pair = state_scr[:, cs]
            s_pair_b = s_pair.astype(BF16)
            y = jnp.zeros((lc, LANES), F32)
            s_new = jnp.zeros((n, LANES), F32)
            for e in range(2):
                h = 2 * p + e
                sel = lo if e == 0 else jnp.logical_not(lo)
                x_h = jnp.where(sel, x_pair, jnp.zeros_like(x_pair))
                s_h = jnp.where(sel, s_pair_b, jnp.zeros_like(s_pair_b))
                acol = jnp.broadcast_to(acum[:, h:h + 1], (lc, lc))
                seg = acol - arow_t[h:h + 1, :]
                decay_dt = jnp.exp2(jnp.where(tril, seg, -1e30))
                m_h = (decay_dt * cb).astype(BF16)
                y = y + jnp.dot(m_h, x_h, preferred_element_type=F32)
                c_h = (cm * jnp.exp2(acol)).astype(BF16)
                y = y + jnp.dot(c_h, s_h, preferred_element_type=F32)
                b_h = (bm_t * w2_t[h:h + 1, :]).astype(BF16)
                s_new = s_new + jnp.dot(b_h, x_h, preferred_element_type=F32)
            d_row = jnp.where(lo, dlast[2 * p:2 * p + 1, :], dlast[2 * p + 1:2 * p + 2, :])
            state_scr[:, cs] = s_pair * d_row + s_new
            y_scr[:, cs] = y + dexp_ref[:, cs] * x_pair.astype(F32)

    zf = z_ref[...].astype(F32)
    yg = y_scr[...] * _silu(zf)
    gw = SSM_D_INNER // SSM_GROUPS
    outs = []
    for g in range(SSM_GROUPS):
        blk = yg[:, g * gw:(g + 1) * gw]
        ms = jnp.mean(blk * blk, axis=-1, keepdims=True)
        outs.append(blk * lax.rsqrt(ms + EPS))
    o_ref[...] = (jnp.concatenate(outs, axis=1) * ng_ref[...]).astype(o_ref.dtype)


def _ssd(proj, dt, dtb, alog, dexp, ng, bsz, seq):
    t = proj.shape[0]
    lc = 8 * SSM_CHUNK
    nc = seq // lc
    xw, bw = SSM_D_INNER, 2 * SSM_GROUPS * SSM_STATE
    row = lambda b, c: b * nc + c
    full = lambda b, c: (0, 0)
    return pl.pallas_call(
        _ssd_kernel,
        grid=(bsz, nc),
        in_specs=[
            pl.BlockSpec((lc, xw), lambda b, c: (row(b, c), COL_Z // xw)),
            pl.BlockSpec((lc, xw), lambda b, c: (row(b, c), COL_XS // xw)),
            pl.BlockSpec((lc, bw), lambda b, c: (row(b, c), COL_BC // bw)),
            pl.BlockSpec((lc, LANES), lambda b, c: (row(b, c), 0)),
            pl.BlockSpec((1, LANES), full),
            pl.BlockSpec((1, LANES), full),
            pl.BlockSpec((1, xw), full),
            pl.BlockSpec((1, xw), full),
        ],
        out_specs=pl.BlockSpec((lc, xw), lambda b, c: (row(b, c), 0)),
        out_shape=jax.ShapeDtypeStruct((t, xw), BF16),
        scratch_shapes=[
            pltpu.VMEM((SSM_STATE, xw), F32),
            pltpu.VMEM((lc, xw), F32),
        ],
        compiler_params=_cparams(("arbitrary", "arbitrary")),
        name="ssd",
    )(proj, proj, proj, dt, dtb, alog, dexp, ng)


def _attn_head_kernel(qt_ref, k_ref, vt_ref, lam_ref, sg_ref, o_ref,
                      qt_scr, m_scr, l_scr, acc_scr, sa_scr, sb_scr,
                      *, tk, nsub, cw, lam_init):
    tq = nsub * tk
    w = 2 * tq
    assert nsub % 2 == 0 and tk % cw == 0
    nq = qt_scr.shape[0]

    def prep_q(u, carry):
        qt = jnp.concatenate([qt_ref[0, nsub * u + i] for i in range(nsub)], axis=1)
        comp = lax.broadcasted_iota(jnp.int32, qt.shape, 0) < DIFF_HEAD_DIM
        zero = jnp.zeros_like(qt)
        qt_scr[u] = jnp.concatenate([jnp.where(comp, qt, zero), jnp.where(comp, zero, qt)], axis=1)
        return carry
    lax.fori_loop(0, nq, prep_q, 0)

    def reset_state():
        m_scr[...] = jnp.full(m_scr.shape, -1e30, F32)
        l_scr[...] = jnp.zeros(l_scr.shape, F32)
        acc_scr[...] = jnp.zeros(acc_scr.shape, F32)

    def k_block(j):
        return k_ref[pl.ds(pl.multiple_of(j * tk, tk), tk), :]

    def scores(u, j, s_ref):
        s_ref[...] = jnp.dot(k_block(j), qt_scr[u], preferred_element_type=F32)

    def chunk_row(c):
        return (c * cw) % tq

    def score_chunks(u, j, s_ref, min_row):
        def one(c):
            cs = slice(c * cw, (c + 1) * cw)
            def run():
                s_ref[:, cs] = jnp.dot(k_block(j), qt_scr[u, :, cs], preferred_element_type=F32)
            return run
        return [one(c) for c in range(w // cw) if chunk_row(c) >= min_row]

    def interleave(first, second):
        for i in range(max(len(first), len(second))):
            if i < len(first):
                first[i]()
            if i < len(second):
                second[i]()

    def update_chunk(s_ref, j, c, nkeys, row_off):
        cs = slice(c * cw, (c + 1) * cw)
        s = s_ref[0:nkeys, cs]
        if row_off is not None:
            key = lax.broadcasted_iota(jnp.int32, (nkeys, cw), 0)
            row = lax.broadcasted_iota(jnp.int32, (nkeys, cw), 1) + row_off
            s = jnp.where(key <= row, s, -1e30)
        m_old = m_scr[:, cs]
        m_new = jnp.maximum(m_old, jnp.max(s, axis=0, keepdims=True))
        alpha = jnp.exp2(m_old - m_new)
        p = jnp.exp2(s - m_new)
        l_scr[:, cs] = alpha * l_scr[:, cs] + jnp.sum(p, axis=0, keepdims=True)
        acc_scr[:, cs] = alpha * acc_scr[:, cs] + jnp.dot(
            vt_ref[0, j, :, 0:nkeys], p.astype(BF16), preferred_element_type=F32)
        m_scr[:, cs] = m_new

    def update_full(j, s_ref):
        return [functools.partial(update_chunk, s_ref, j, c, tk, None) for c in range(w // cw)]

    def update_diag(j, s_ref, i):
        out = []
        for c in range(w // cw):
            row_off = chunk_row(c) - i * tk
            if 0 <= row_off < tk:
                out.append(functools.partial(update_chunk, s_ref, j, c, row_off + cw, row_off))
            elif row_off >= tk:
                out.append(functools.partial(update_chunk, s_ref, j, c, tk, None))
        return out

    def finalize(u):
        lam = (jnp.exp(jnp.sum(lam_ref[0:1, :] * lam_ref[1:2, :], axis=-1, keepdims=True))
               - jnp.exp(jnp.sum(lam_ref[2:3, :] * lam_ref[3:4, :], axis=-1, keepdims=True))
               + lam_init)
        ot = acc_scr[...] / l_scr[...]
        o = (ot[:, :tq] - lam * ot[:, tq:]).T
        ms = jnp.mean(o * o, axis=-1, keepdims=True)
        o = o * lax.rsqrt(ms + EPS) * sg_ref[...] * (1.0 - lam_init)
        o_ref[pl.ds(pl.multiple_of(u * tq, tq), tq), :] = o.astype(o_ref.dtype)

    reset_state()
    scores(0, 0, sa_scr)

    def tile(u, carry):
        def pair(j):
            interleave(score_chunks(u, j + 1, sb_scr, 0), update_full(j, sa_scr))
            interleave(score_chunks(u, j + 2, sa_scr, 0), update_full(j + 1, sb_scr))

        npairs = (nsub // 2) * u

        def two_pairs(t2, c2):
            pair(4 * t2)
            pair(4 * t2 + 2)
            return c2
        lax.fori_loop(0, npairs // 2, two_pairs, 0)

        @pl.when(npairs % 2 == 1)
        def _():
            pair(2 * (npairs - 1))

        jd = nsub * u
        u_next = jnp.minimum(u + 1, nq - 1)
        for i in range(nsub):
            cur, nxt = (sa_scr, sb_scr) if i % 2 == 0 else (sb_scr, sa_scr)
            if i + 1 < nsub:
                ahead = score_chunks(u, jd + i + 1, nxt, (i + 1) * tk)
            else:
                ahead = score_chunks(u_next, 0, nxt, 0)
            interleave(ahead, update_diag(jd + i, cur, i))
        finalize(u)
        reset_state()
        return carry

    lax.fori_loop(0, nq, tile, 0)


def _attn_head(proj, qt, vt, lam4, sg, bsz, seq, lam_init):
    t = proj.shape[0]
    nkv, tk = qt.shape[1], qt.shape[3]
    nsub = 2
    tq = nsub * tk
    cw = 256
    kern = functools.partial(_attn_head_kernel, tk=tk, nsub=nsub, cw=cw, lam_init=lam_init)
    return pl.pallas_call(
        kern,
        grid=(bsz, DIFF_HEADS),
        in_specs=[
            pl.BlockSpec((1, nkv, LANES, tk), lambda b, h: (b, 0, h, 0)),
            pl.BlockSpec((seq, LANES), lambda b, h: (b, OUT_K // LANES + h)),
            pl.BlockSpec((1, nkv, LANES, tk), lambda b, h: (b, 0, h, 0)),
            pl.BlockSpec((4, DIFF_HEAD_DIM), lambda b, h: (0, 0)),
            pl.BlockSpec((1, LANES), lambda b, h: (0, 0)),
        ],
        out_specs=pl.BlockSpec((seq, LANES), lambda b, h: (b, h)),
        out_shape=jax.ShapeDtypeStruct((t, DIFF_HEADS * LANES), BF16),
        scratch_shapes=[
            pltpu.VMEM((seq // tq, LANES, 2 * tq), BF16),
            pltpu.VMEM((1, 2 * tq), F32),
            pltpu.VMEM((1, 2 * tq), F32),
            pltpu.VMEM((LANES, 2 * tq), F32),
            pltpu.VMEM((tk, 2 * tq), F32),
            pltpu.VMEM((tk, 2 * tq), F32),
        ],
        compiler_params=_cparams(("arbitrary", "arbitrary")),
        name="attn",
    )(qt, proj, vt, lam4, sg)


def _merge_kernel(x_ref, ys_ref, ya_ref, gs_ref, ga_ref, g1_ref, ws_ref, wa_ref, wo_ref, o_ref):
    ys = jnp.dot(ys_ref[...], ws_ref[...], preferred_element_type=F32)
    ya = jnp.dot(ya_ref[...], wa_ref[...], preferred_element_type=F32)
    mixed = (jax.nn.sigmoid(gs_ref[...].astype(F32)) * ys
             + jax.nn.sigmoid(ga_ref[...].astype(F32)) * ya)
    upd = jnp.dot(mixed.astype(BF16), wo_ref[...], preferred_element_type=F32)
    o_ref[...] = x_ref[...] + g1_ref[0] * upd


def _merge(x2, yssm, yatt, proj, g1, ws, wa, wo, seq):
    t, d = x2.shape
    tm = 512
    nps = seq // tm
    const = lambda i: (0, 0)
    return pl.pallas_call(
        _merge_kernel,
        grid=(t // tm,),
        in_specs=[
            pl.BlockSpec((tm, d), lambda i: (i, 0)),
            pl.BlockSpec((tm, SSM_D_INNER), lambda i: (i, 0)),
            pl.BlockSpec((tm, d), lambda i: (i, 0)),
            pl.BlockSpec((tm, d), lambda i: (i, OUT_GS // d)),
            pl.BlockSpec((tm, d), lambda i: (i, OUT_GA // d)),
            pl.BlockSpec((1, 1, d), lambda i: (i // nps, 0, 0)),
            pl.BlockSpec((SSM_D_INNER, d), const),
            pl.BlockSpec((d, d), const),
            pl.BlockSpec((d, d), const),
        ],
        out_specs=pl.BlockSpec((tm, d), lambda i: (i, 0)),
        out_shape=jax.ShapeDtypeStruct((t, d), F32),
        compiler_params=_cparams(("arbitrary",)),
        name="merge",
    )(x2, yssm, yatt, proj, proj, g1, ws, wa, wo)


def _mlp_kernel(x_ref, g_ref, sc_ref, sh_ref, g2_ref, w1_ref, w2_ref, fg_ref, o_ref, *, final):
    xf = x_ref[...]
    ms = jnp.mean(xf * xf, axis=-1, keepdims=True)
    h = xf * lax.rsqrt(ms + EPS) * g_ref[...]
    h = h * (1.0 + sc_ref[0]) + sh_ref[0]
    u = jnp.dot(h.astype(BF16), w1_ref[...], preferred_element_type=F32)
    u = jnp.square(jnp.maximum(u, 0.0))
    y = xf + g2_ref[0] * jnp.dot(u.astype(BF16), w2_ref[...], preferred_element_type=F32)
    if final:
        ms2 = jnp.mean(y * y, axis=-1, keepdims=True)
        y = y * lax.rsqrt(ms2 + EPS) * fg_ref[...]
    o_ref[...] = y


def _mlp(x2, g, sc, sh, g2, w1, w2, fg, seq, final):
    t, d = x2.shape
    tm = 512
    nps = seq // tm
    const = lambda i: (0, 0)
    per_b = lambda i: (i // nps, 0, 0)
    return pl.pallas_call(
        functools.partial(_mlp_kernel, final=final),
        grid=(t // tm,),
        in_specs=[
            pl.BlockSpec((tm, d), lambda i: (i, 0)),
            pl.BlockSpec((1, d), const),
            pl.BlockSpec((1, 1, d), per_b),
            pl.BlockSpec((1, 1, d), per_b),
            pl.BlockSpec((1, 1, d), per_b),
            pl.BlockSpec((d, D_FF), const),
            pl.BlockSpec((D_FF, d), const),
            pl.BlockSpec((1, d), const),
        ],
        out_specs=pl.BlockSpec((tm, d), lambda i: (i, 0)),
        out_shape=jax.ShapeDtypeStruct((t, d), F32),
        compiler_params=_cparams(("arbitrary",)),
        name="mlp",
    )(x2, g, sc, sh, g2, w1, w2, fg)


def _rope_tables(seq):
    half = DIFF_HEAD_DIM // 2
    inv = 1.0 / (ROPE_THETA ** (jnp.arange(0, DIFF_HEAD_DIM, 2, dtype=F32) / DIFF_HEAD_DIM))
    ang = jnp.arange(seq, dtype=F32)[:, None] * inv[None, :]
    cos, sin = jnp.cos(ang), jnp.sin(ang)
    reps = LANES // half
    cos_t = jnp.tile(cos, (1, reps))
    sin_t = jnp.tile(jnp.concatenate([-sin, sin], axis=1), (1, reps // 2))
    return cos_t, sin_t


def _lambda_init(layer_idx):
    return 0.8 - 0.6 * math.exp(-0.3 * layer_idx)


def kernel(x, c, w_ada, b_ada, norm1_g, w_in, conv_w, conv_b, dt_bias, a_log, d_skip,
           ssm_norm_g, lam_q1, lam_k1, lam_q2, lam_k2, subln_g, w_ssm_out, w_attn_out,
           w_out, norm2_g, w_ff1, w_ff2, final_g):
    bsz, seq, d = x.shape
    depth = w_in.shape[0]
    assert d == D_MODEL and seq % 1024 == 0
    t = bsz * seq
    cos_t, sin_t = _rope_tables(seq)
    ada = _ada(c, w_ada, b_ada)
    x2 = x.reshape(t, d)
    pad_h = LANES - SSM_HEADS
    for l in range(depth):
        mods = [ada[l, :, i * d:(i + 1) * d].reshape(bsz, 1, d) for i in range(N_ADA)]
        sh1, sc1, g1, sh2, sc2, g2 = mods
        w_l = w_in[l]
        w_main = jnp.concatenate([w_l[:, :DT_OFF], w_l[:, DT_OFF + DT_W:]], axis=1).astype(BF16)
        w_dt = jnp.pad(w_l[:, DT_OFF:DT_OFF + DT_W], ((0, 0), (0, pad_h))).astype(BF16)
        proj, dt, qt, vt = _inproj(x2, norm1_g[l].reshape(1, d), sc1, sh1, w_main, w_dt,
                           cos_t, sin_t, conv_w[l], conv_b[l].reshape(1, -1), seq)
        yssm = _ssd(
            proj, dt,
            jnp.pad(dt_bias[l], (0, pad_h)).reshape(1, LANES),
            jnp.pad(a_log[l], (0, pad_h)).reshape(1, LANES),
            jnp.repeat(d_skip[l], SSM_HEAD_DIM).reshape(1, SSM_D_INNER),
            ssm_norm_g[l].reshape(1, SSM_D_INNER), bsz, seq)
        lam4 = jnp.stack([lam_q1[l], lam_k1[l], lam_q2[l], lam_k2[l]], axis=0)
        yatt = _attn_head(proj, qt, vt, lam4, subln_g[l].reshape(1, LANES), bsz, seq, _lambda_init(l))
        x2 = _merge(x2, yssm, yatt, proj, g1, w_ssm_out[l].astype(BF16),
                    w_attn_out[l].astype(BF16), w_out[l].astype(BF16), seq)
        x2 = _mlp(x2, norm2_g[l].reshape(1, d), sc2, sh2, g2, w_ff1[l].astype(BF16),
                  w_ff2[l].astype(BF16), final_g.reshape(1, d), seq, l == depth - 1)
    return x2.reshape(bsz, seq, d)
```

```python
import functools
import math

import jax
import jax.numpy as jnp
from jax import lax
from jax.experimental import pallas as pl
from jax.experimental.pallas import tpu as pltpu

F32 = jnp.float32
BF16 = jnp.bfloat16
HIGHEST = lax.Precision.HIGHEST

D_MODEL = 1024
SSM_D_INNER = 2048
SSM_HEAD_DIM = 64
SSM_HEADS = 32
SSM_GROUPS = 4
SSM_STATE = 128
SSM_CONV = 4
SSM_CHUNK = 128
DIFF_HEADS = 8
DIFF_HEAD_DIM = 64
D_FF = 4096
N_ADA = 6
EPS = 1e-5
ROPE_THETA = 10000.0
DT_OFF = 5120
DT_W = SSM_HEADS
LOG2E = 1.4426950408889634

LANES = 128
HALO = 8
VMEM_LIMIT = 56 * 1024 * 1024

COL_Z, COL_XS, COL_BC, COL_Q, COL_K, COL_V, COL_GS, COL_GA = (
    0, 2048, 4096, 5120, 6144, 7168, 8192, 9216)
PROJ_W = 10240
OUT_K, OUT_GS, OUT_GA = 5120, 6144, 7168
OUT_W = 8192


def _out_col(col):
    assert not (COL_Q <= col < COL_K or COL_V <= col < COL_GS)
    if col < COL_Q:
        return col
    return col - (COL_K - OUT_K) if col < COL_V else col - (COL_GS - OUT_GS)


def _silu(x):
    h = 0.5 * x
    return h + h * jnp.tanh(h)


def _cparams(sem):
    return pltpu.CompilerParams(dimension_semantics=sem, vmem_limit_bytes=VMEM_LIMIT)


def _ada_kernel(c_ref, w_ref, b_ref, o_ref):
    c = c_ref[...]
    ca = c * jax.nn.sigmoid(c)
    o_ref[0] = jnp.dot(ca, w_ref[0], precision=HIGHEST,
                       preferred_element_type=F32) + b_ref[0]


def _ada(c, w_ada, b_ada):
    depth, d, n = w_ada.shape
    bsz = c.shape[0]
    tn = 1024
    return pl.pallas_call(
        _ada_kernel,
        grid=(depth, n // tn),
        in_specs=[
            pl.BlockSpec((bsz, d), lambda l, j: (0, 0)),
            pl.BlockSpec((1, d, tn), lambda l, j: (l, 0, j)),
            pl.BlockSpec((1, 1, tn), lambda l, j: (l, 0, j)),
        ],
        out_specs=pl.BlockSpec((1, bsz, tn), lambda l, j: (l, 0, j)),
        out_shape=jax.ShapeDtypeStruct((depth, bsz, n), F32),
        compiler_params=_cparams(("arbitrary", "arbitrary")),
        name="ada",
    )(c, w_ada, b_ada.reshape(depth, 1, n))


def _rope_tile(acc, cos, sin, scale):
    half = DIFF_HEAD_DIM // 2
    lane = lax.broadcasted_iota(jnp.int32, cos.shape, 1)
    first_half = (lane % DIFF_HEAD_DIM) < half
    outs = []
    for hh in range(acc.shape[1] // LANES):
        t = acc[:, hh * LANES:(hh + 1) * LANES]
        sw = jnp.where(first_half, pltpu.roll(t, LANES - half, 1), pltpu.roll(t, half, 1))
        outs.append((t * cos + sw * sin) * scale)
    return jnp.concatenate(outs, axis=1)


def _conv_silu_tile(pad_scr, hist_scr, cw_ref, cb_ref, ccs):
    tm = pad_scr.shape[0] - HALO
    nh = SSM_CONV - 1
    pad_scr[HALO - nh:HALO, :] = hist_scr[HALO - nh:HALO, ccs]
    y = cb_ref[:, ccs] + cw_ref[nh:nh + 1, ccs] * pad_scr[HALO:HALO + tm, :]
    for k in range(nh):
        off = HALO - nh + k
        y = y + cw_ref[k:k + 1, ccs] * pad_scr[off:off + tm, :]
    hist_scr[HALO - nh:HALO, ccs] = pad_scr[HALO + tm - nh:HALO + tm, :]
    return _silu(y)


def _inproj_kernel(x_ref, g_ref, sc_ref, sh_ref, w_ref, wdt_ref, cos_ref, sin_ref,
                   cw_ref, cb_ref, o_ref, dt_ref, qt_ref, vt_ref, pad_scr, hist_scr,
                   *, tn, nps, qscale):
    @pl.when(pl.program_id(0) % nps == 0)
    def _():
        hist_scr[...] = jnp.zeros_like(hist_scr)

    xf = x_ref[...]
    ms = jnp.mean(xf * xf, axis=-1, keepdims=True)
    h = xf * lax.rsqrt(ms + EPS) * g_ref[...]
    h = h * (1.0 + sc_ref[0]) + sh_ref[0]
    hb = h.astype(BF16)
    dt_ref[...] = jnp.dot(hb, wdt_ref[...], preferred_element_type=F32)
    cols = list(range(0, PROJ_W, tn))
    heavy = [c for c in cols if COL_XS <= c < COL_GS]
    light = [c for c in cols if not COL_XS <= c < COL_GS]
    order = []
    while heavy or light:
        if heavy:
            order.append(heavy.pop(0))
        if light:
            order.append(light.pop(0))
    tm = x_ref.shape[0]
    pending = None
    n_staged = 0
    for col in order + [None]:
        staged = None
        if col is not None:
            cs = slice(col, col + tn)
            acc = jnp.dot(hb, w_ref[:, cs], preferred_element_type=F32)
            if COL_XS <= col < COL_GS:
                slot = n_staged % pad_scr.shape[0]
                n_staged += 1
                pad_scr[slot, HALO:HALO + tm, :] = acc
                staged = (slot, col)
            else:
                ocol = _out_col(col)
                o_ref[:, ocol:ocol + tn] = acc.astype(o_ref.dtype)
        if pending is not None:
            slot, pcol = pending
            if pcol < COL_Q:
                y = _conv_silu_tile(pad_scr.at[slot], hist_scr, cw_ref, cb_ref,
                                    slice(pcol - COL_XS, pcol - COL_XS + tn))
            elif pcol < COL_V:
                y = _rope_tile(pad_scr[slot, HALO:HALO + tm, :], cos_ref[...], sin_ref[...],
                               qscale if pcol < COL_K else 1.0)
            else:
                y = pad_scr[slot, HALO:HALO + tm, :]
            if COL_Q <= pcol < COL_K:
                qt_ref[0, 0, pcol - COL_Q:pcol - COL_Q + tn, :] = y.astype(BF16).T
            elif COL_V <= pcol:
                vt_ref[0, 0, pcol - COL_V:pcol - COL_V + tn, :] = y.astype(BF16).T
            else:
                ocol = _out_col(pcol)
                o_ref[:, ocol:ocol + tn] = y.astype(o_ref.dtype)
        pending = staged


def _inproj(x2, g, sc, sh, w_main, w_dt, cos_t, sin_t, conv_w, conv_b, seq):
    t, d = x2.shape
    tm, tn = 512, 256
    nps = seq // tm
    conv_dim = conv_w.shape[1]
    qscale = (DIFF_HEAD_DIM ** -0.5) * LOG2E
    kern = functools.partial(_inproj_kernel, tn=tn, nps=nps, qscale=qscale)
    resident = dict(pipeline_mode=pl.Buffered(1))
    return pl.pallas_call(
        kern,
        grid=(t // tm,),
        in_specs=[
            pl.BlockSpec((tm, d), lambda i: (i, 0)),
            pl.BlockSpec((1, d), lambda i: (0, 0)),
            pl.BlockSpec((1, 1, d), lambda i: (i // nps, 0, 0)),
            pl.BlockSpec((1, 1, d), lambda i: (i // nps, 0, 0)),
            pl.BlockSpec((d, PROJ_W), lambda i: (0, 0), **resident),
            pl.BlockSpec((d, LANES), lambda i: (0, 0), **resident),
            pl.BlockSpec((tm, LANES), lambda i: (i % nps, 0)),
            pl.BlockSpec((tm, LANES), lambda i: (i % nps, 0)),
            pl.BlockSpec((SSM_CONV, conv_dim), lambda i: (0, 0)),
            pl.BlockSpec((1, conv_dim), lambda i: (0, 0)),
        ],
        out_specs=[
            pl.BlockSpec((tm, OUT_W), lambda i: (i, 0)),
            pl.BlockSpec((tm, LANES), lambda i: (i, 0)),
            pl.BlockSpec((1, 1, DIFF_HEADS * LANES, tm), lambda i: (i // nps, i % nps, 0, 0)),
            pl.BlockSpec((1, 1, DIFF_HEADS * LANES, tm), lambda i: (i // nps, i % nps, 0, 0)),
        ],
        out_shape=[
            jax.ShapeDtypeStruct((t, OUT_W), BF16),
            jax.ShapeDtypeStruct((t, LANES), F32),
            jax.ShapeDtypeStruct((t // seq, nps, DIFF_HEADS * LANES, tm), BF16),
            jax.ShapeDtypeStruct((t // seq, nps, DIFF_HEADS * LANES, tm), BF16),
        ],
        scratch_shapes=[
            pltpu.VMEM((4, HALO + tm, tn), F32),
            pltpu.VMEM((HALO, conv_dim), F32),
        ],
        compiler_params=_cparams(("arbitrary",)),
        name="inproj",
    )(x2, g, sc, sh, w_main, w_dt, cos_t, sin_t, conv_w, conv_b)


def _softplus(x):
    e = jnp.exp(-jnp.abs(x))
    u = 1.0 + e
    log1p_e = jnp.where(u == 1.0, e, jnp.log(u) * (e / (u - 1.0)))
    return jnp.maximum(x, 0.0) + log1p_e


def _split3(x):
    p0 = x.astype(BF16)
    r1 = x - p0.astype(F32)
    p1 = r1.astype(BF16)
    p2 = (r1 - p1.astype(F32)).astype(BF16)
    return p0, p1, p2


def _cumsum_rows(tri, x):
    return sum(jnp.dot(tri, p, preferred_element_type=F32) for p in _split3(x))


def _cumsum_cols(x, tri):
    return sum(jnp.dot(p, tri, preferred_element_type=F32) for p in _split3(x))


def _ssd_kernel(z_ref, xs_ref, bc_ref, dt_ref, dtb_ref, alog_ref, dexp_ref, ng_ref, o_ref,
                state_scr, y_scr):
    @pl.when(pl.program_id(1) == 0)
    def _():
        state_scr[...] = jnp.zeros_like(state_scr)

    for ci in range(z_ref.shape[0] // SSM_CHUNK):
        rows = pl.ds(ci * SSM_CHUNK, SSM_CHUNK)
        _ssd_chunk(z_ref.at[rows], xs_ref.at[rows], bc_ref.at[rows], dt_ref.at[rows],
                   dtb_ref, alog_ref, dexp_ref, ng_ref, o_ref.at[rows], state_scr,
                   y_scr.at[rows])


def _ssd_chunk(z_ref, xs_ref, bc_ref, dt_ref, dtb_ref, alog_ref, dexp_ref, ng_ref, o_ref,
               state_scr, y_scr):
    lc = SSM_CHUNK
    n = SSM_STATE
    xs_b = xs_ref[...]
    bc = bc_ref[...].astype(F32)

    dtp = _softplus(dt_ref[...] + dtb_ref[...])
    a_step = dtp * (-LOG2E * jnp.exp(alog_ref[...]))
    row_i = lax.broadcasted_iota(jnp.int32, (lc, lc), 0)
    col_i = lax.broadcasted_iota(jnp.int32, (lc, lc), 1)
    tril = row_i >= col_i
    tril_b = jnp.where(tril, 1.0, 0.0).astype(BF16)
    triu_b = jnp.where(row_i <= col_i, 1.0, 0.0).astype(BF16)
    acum = _cumsum_rows(tril_b, a_step)
    nh8 = SSM_HEADS
    a_step_t = a_step.T[:nh8, :]
    dt_t = dtp.T[:nh8, :]
    acum_t = _cumsum_cols(a_step_t, triu_b)
    alast_t = acum_t[:, lc - 1:lc]
    w2_t = dt_t * jnp.exp2(alast_t - acum_t)
    dlast = jnp.broadcast_to(jnp.exp2(alast_t), (nh8, LANES))
    arow_t = acum_t - jnp.log2(dt_t)

    lane = lax.broadcasted_iota(jnp.int32, (1, LANES), 1)
    lo = lane < SSM_HEAD_DIM
    heads_per_group = SSM_HEADS // SSM_GROUPS
    for g in range(SSM_GROUPS):
        bm = bc[:, g * n:(g + 1) * n]
        cm = bc[:, SSM_GROUPS * n + g * n:SSM_GROUPS * n + (g + 1) * n]
        bm_t = bm.T
        cb = jnp.dot(cm.astype(BF16), bm_t.astype(BF16), preferred_element_type=F32)
        for pp in range(heads_per_group // 2):
            p = g * (heads_per_group // 2) + pp
            cs = slice(p * LANES, (p + 1) * LANES)
            x_pair = xs_b[:, cs]
            s_pair = state_scr[:, cs]
            s_pair_b = s_pair.astype(BF16)
            y = jnp.zeros((lc, LANES), F32)
            s_new = jnp.zeros((n, LANES), F32)
            for e in range(2):
                h = 2 * p + e
                sel = lo if e == 0 else jnp.logical_not(lo)
                x_h = jnp.where(sel, x_pair, jnp.zeros_like(x_pair))
                s_h = jnp.where(sel, s_pair_b, jnp.zeros_like(s_pair_b))
                acol = jnp.broadcast_to(acum[:, h:h + 1], (lc, lc))
                seg = acol - arow_t[h:h + 1, :]
                decay_dt = jnp.exp2(jnp.where(tril, seg, -1e30))
                m_h = (decay_dt * cb).astype(BF16)
                y = y + jnp.dot(m_h, x_h, preferred_element_type=F32)
                c_h = (cm * jnp.exp2(acol)).astype(BF16)
                y = y + jnp.dot(c_h, s_h, preferred_element_type=F32)
                b_h = (bm_t * w2_t[h:h + 1, :]).astype(BF16)
                s_new = s_new + jnp.dot(b_h, x_h, preferred_element_type=F32)
            d_row = jnp.where(lo, dlast[2 * p:2 * p + 1, :], dlast[2 * p + 1:2 * p + 2, :])
            state_scr[:, cs] = s_pair * d_row + s_new
            y_scr[:, cs] = y + dexp_ref[:, cs] * x_pair.astype(F32)

    zf = z_ref[...].astype(F32)
    yg = y_scr[...] * _silu(zf)
    gw = SSM_D_INNER // SSM_GROUPS
    outs = []
    for g in range(SSM_GROUPS):
        blk = yg[:, g * gw:(g + 1) * gw]
        ms = jnp.mean(blk * blk, axis=-1, keepdims=True)
        outs.append(blk * lax.rsqrt(ms + EPS))
    o_ref[...] = (jnp.concatenate(outs, axis=1) * ng_ref[...]).astype(o_ref.dtype)


def _ssd(proj, dt, dtb, alog, dexp, ng, bsz, seq):
    t = proj.shape[0]
    lc = 8 * SSM_CHUNK
    nc = seq // lc
    xw, bw = SSM_D_INNER, 2 * SSM_GROUPS * SSM_STATE
    row = lambda b, c: b * nc + c
    full = lambda b, c: (0, 0)
    return pl.pallas_call(
        _ssd_kernel,
        grid=(bsz, nc),
        in_specs=[
            pl.BlockSpec((lc, xw), lambda b, c: (row(b, c), COL_Z // xw)),
            pl.BlockSpec((lc, xw), lambda b, c: (row(b, c), COL_XS // xw)),
            pl.BlockSpec((lc, bw), lambda b, c: (row(b, c), COL_BC // bw)),
            pl.BlockSpec((lc, LANES), lambda b, c: (row(b, c), 0)),
            pl.BlockSpec((1, LANES), full),
            pl.BlockSpec((1, LANES), full),
            pl.BlockSpec((1, xw), full),
            pl.BlockSpec((1, xw), full),
        ],
        out_specs=pl.BlockSpec((lc, xw), lambda b, c: (row(b, c), 0)),
        out_shape=jax.ShapeDtypeStruct((t, xw), BF16),
        scratch_shapes=[
            pltpu.VMEM((SSM_STATE, xw), F32),
            pltpu.VMEM((lc, xw), F32),
        ],
        compiler_params=_cparams(("arbitrary", "arbitrary")),
        name="ssd",
    )(proj, proj, proj, dt, dtb, alog, dexp, ng)


def _attn_head_kernel(qt_ref, k_ref, vt_ref, lam_ref, sg_ref, o_ref,
                      qt_scr, m_scr, l_scr, acc_scr, sa_scr, sb_scr,
                      *, tk, nsub, cw, lam_init):
    tq = nsub * tk
    w = 2 * tq
    assert nsub % 2 == 0 and tk % cw == 0
    nq = qt_scr.shape[0]

    def prep_q(u, carry):
        qt = jnp.concatenate([qt_ref[0, nsub * u + i] for i in range(nsub)], axis=1)
        comp = lax.broadcasted_iota(jnp.int32, qt.shape, 0) < DIFF_HEAD_DIM
        zero = jnp.zeros_like(qt)
        qt_scr[u] = jnp.concatenate([jnp.where(comp, qt, zero), jnp.where(comp, zero, qt)], axis=1)
        return carry
    lax.fori_loop(0, nq, prep_q, 0)

    def reset_state():
        m_scr[...] = jnp.full(m_scr.shape, -1e30, F32)
        l_scr[...] = jnp.zeros(l_scr.shape, F32)
        acc_scr[...] = jnp.zeros(acc_scr.shape, F32)

    def k_block(j):
        return k_ref[pl.ds(pl.multiple_of(j * tk, tk), tk), :]

    def scores(u, j, s_ref):
        s_ref[...] = jnp.dot(k_block(j), qt_scr[u], preferred_element_type=F32)

    def chunk_row(c):
        return (c * cw) % tq

    def score_chunks(u, j, s_ref, min_row):
        def one(c):
            cs = slice(c * cw, (c + 1) * cw)
            def run():
                s_ref[:, cs] = jnp.dot(k_block(j), qt_scr[u, :, cs], preferred_element_type=F32)
            return run
        return [one(c) for c in range(w // cw) if chunk_row(c) >= min_row]

    def interleave(first, second):
        for i in range(max(len(first), len(second))):
            if i < len(first):
                first[i]()
            if i < len(second):
                second[i]()

    def update_chunk(s_ref, j, c, nkeys, row_off):
        cs = slice(c * cw, (c + 1) * cw)
        s = s_ref[0:nkeys, cs]
        if row_off is not None:
            key = lax.broadcasted_iota(jnp.int32, (nkeys, cw), 0)
            row = lax.broadcasted_iota(jnp.int32, (nkeys, cw), 1) + row_off
            s = jnp.where(key <= row, s, -1e30)
        m_old = m_scr[:, cs]
        m_new = jnp.maximum(m_old, jnp.max(s, axis=0, keepdims=True))
        alpha = jnp.exp2(m_old - m_new)
        p = jnp.exp2(s - m_new)
        l_scr[:, cs] = alpha * l_scr[:, cs] + jnp.sum(p, axis=0, keepdims=True)
        acc_scr[:, cs] = alpha * acc_scr[:, cs] + jnp.dot(
            vt_ref[0, j, :, 0:nkeys], p.astype(BF16), preferred_element_type=F32)
        m_scr[:, cs] = m_new

    def update_full(j, s_ref):
        return [functools.partial(update_chunk, s_ref, j, c, tk, None) for c in range(w // cw)]

    def update_diag(j, s_ref, i):
        out = []
        for c in range(w // cw):
            row_off = chunk_row(c) - i * tk
            if 0 <= row_off < tk:
                out.append(functools.partial(update_chunk, s_ref, j, c, row_off + cw, row_off))
            elif row_off >= tk:
                out.append(functools.partial(update_chunk, s_ref, j, c, tk, None))
        return out

    def finalize(u):
        lam = (jnp.exp(jnp.sum(lam_ref[0:1, :] * lam_ref[1:2, :], axis=-1, keepdims=True))
               - jnp.exp(jnp.sum(lam_ref[2:3, :] * lam_ref[3:4, :], axis=-1, keepdims=True))
               + lam_init)
        ot = acc_scr[...] / l_scr[...]
        o = (ot[:, :tq] - lam * ot[:, tq:]).T
        ms = jnp.mean(o * o, axis=-1, keepdims=True)
        o = o * lax.rsqrt(ms + EPS) * sg_ref[...] * (1.0 - lam_init)
        o_ref[pl.ds(pl.multiple_of(u * tq, tq), tq), :] = o.astype(o_ref.dtype)

    reset_state()
    scores(0, 0, sa_scr)

    def tile(u, carry):
        def pair(j):
            interleave(score_chunks(u, j + 1, sb_scr, 0), update_full(j, sa_scr))
            interleave(score_chunks(u, j + 2, sa_scr, 0), update_full(j + 1, sb_scr))

        npairs = (nsub // 2) * u

        def two_pairs(t2, c2):
            pair(4 * t2)
            pair(4 * t2 + 2)
            return c2
        lax.fori_loop(0, npairs // 2, two_pairs, 0)

        def diagonal():
            jd = nsub * u
            u_next = jnp.minimum(u + 1, nq - 1)
            for i in range(nsub):
                cur, nxt = (sa_scr, sb_scr) if i % 2 == 0 else (sb_scr, sa_scr)
                if i + 1 < nsub:
                    ahead = score_chunks(u, jd + i + 1, nxt, (i + 1) * tk)
                else:
                    ahead = score_chunks(u_next, 0, nxt, 0)
                interleave(ahead, update_diag(jd + i, cur, i))
            finalize(u)
            reset_state()

        @pl.when(npairs % 2 == 1)
        def _():
            pair(2 * (npairs - 1))
            diagonal()

        @pl.when(npairs % 2 == 0)
        def _():
            diagonal()
        return carry

    lax.fori_loop(0, nq, tile, 0)


def _attn_head(proj, qt, vt, lam4, sg, bsz, seq, lam_init):
    t = proj.shape[0]
    nkv, tk = qt.shape[1], qt.shape[3]
    nsub = 2
    tq = nsub * tk
    cw = 256
    kern = functools.partial(_attn_head_kernel, tk=tk, nsub=nsub, cw=cw, lam_init=lam_init)
    return pl.pallas_call(
        kern,
        grid=(bsz, DIFF_HEADS),
        in_specs=[
            pl.BlockSpec((1, nkv, LANES, tk), lambda b, h: (b, 0, h, 0)),
            pl.BlockSpec((seq, LANES), lambda b, h: (b, OUT_K // LANES + h)),
            pl.BlockSpec((1, nkv, LANES, tk), lambda b, h: (b, 0, h, 0)),
            pl.BlockSpec((4, DIFF_HEAD_DIM), lambda b, h: (0, 0)),
            pl.BlockSpec((1, LANES), lambda b, h: (0, 0)),
        ],
        out_specs=pl.BlockSpec((seq, LANES), lambda b, h: (b, h)),
        out_shape=jax.ShapeDtypeStruct((t, DIFF_HEADS * LANES), BF16),
        scratch_shapes=[
            pltpu.VMEM((seq // tq, LANES, 2 * tq), BF16),
            pltpu.VMEM((1, 2 * tq), F32),
            pltpu.VMEM((1, 2 * tq), F32),
            pltpu.VMEM((LANES, 2 * tq), F32),
            pltpu.VMEM((tk, 2 * tq), F32),
            pltpu.VMEM((tk, 2 * tq), F32),
        ],
        compiler_params=_cparams(("arbitrary", "arbitrary")),
        name="attn",
    )(qt, proj, vt, lam4, sg)


def _merge_kernel(x_ref, ys_ref, ya_ref, gs_ref, ga_ref, g1_ref, ws_ref, wa_ref, wo_ref, o_ref):
    ys = jnp.dot(ys_ref[...], ws_ref[...], preferred_element_type=F32)
    ya = jnp.dot(ya_ref[...], wa_ref[...], preferred_element_type=F32)
    mixed = (jax.nn.sigmoid(gs_ref[...].astype(F32)) * ys
             + jax.nn.sigmoid(ga_ref[...].astype(F32)) * ya)
    upd = jnp.dot(mixed.astype(BF16), wo_ref[...], preferred_element_type=F32)
    o_ref[...] = x_ref[...] + g1_ref[0] * upd


def _merge(x2, yssm, yatt, proj, g1, ws, wa, wo, seq):
    t, d = x2.shape
    tm = 512
    nps = seq // tm
    const = lambda i: (0, 0)
    return pl.pallas_call(
        _merge_kernel,
        grid=(t // tm,),
        in_specs=[
            pl.BlockSpec((tm, d), lambda i: (i, 0)),
            pl.BlockSpec((tm, SSM_D_INNER), lambda i: (i, 0)),
            pl.BlockSpec((tm, d), lambda i: (i, 0)),
            pl.BlockSpec((tm, d), lambda i: (i, OUT_GS // d)),
            pl.BlockSpec((tm, d), lambda i: (i, OUT_GA // d)),
            pl.BlockSpec((1, 1, d), lambda i: (i // nps, 0, 0)),
            pl.BlockSpec((SSM_D_INNER, d), const),
            pl.BlockSpec((d, d), const),
            pl.BlockSpec((d, d), const),
        ],
        out_specs=pl.BlockSpec((tm, d), lambda i: (i, 0)),
        out_shape=jax.ShapeDtypeStruct((t, d), F32),
        compiler_params=_cparams(("arbitrary",)),
        name="merge",
    )(x2, yssm, yatt, proj, proj, g1, ws, wa, wo)


def _mlp_kernel(x_ref, g_ref, sc_ref, sh_ref, g2_ref, w1_ref, w2_ref, fg_ref, o_ref, *, final):
    xf = x_ref[...]
    ms = jnp.mean(xf * xf, axis=-1, keepdims=True)
    h = xf * lax.rsqrt(ms + EPS) * g_ref[...]
    h = h * (1.0 + sc_ref[0]) + sh_ref[0]
    u = jnp.dot(h.astype(BF16), w1_ref[...], preferred_element_type=F32)
    u = jnp.square(jnp.maximum(u, 0.0))
    y = xf + g2_ref[0] * jnp.dot(u.astype(BF16), w2_ref[...], preferred_element_type=F32)
    if final:
        ms2 = jnp.mean(y * y, axis=-1, keepdims=True)
        y = y * lax.rsqrt(ms2 + EPS) * fg_ref[...]
    o_ref[...] = y


def _mlp(x2, g, sc, sh, g2, w1, w2, fg, seq, final):
    t, d = x2.shape
    tm = 512
    nps = seq // tm
    const = lambda i: (0, 0)
    per_b = lambda i: (i // nps, 0, 0)
    return pl.pallas_call(
        functools.partial(_mlp_kernel, final=final),
        grid=(t // tm,),
        in_specs=[
            pl.BlockSpec((tm, d), lambda i: (i, 0)),
            pl.BlockSpec((1, d), const),
            pl.BlockSpec((1, 1, d), per_b),
            pl.BlockSpec((1, 1, d), per_b),
            pl.BlockSpec((1, 1, d), per_b),
            pl.BlockSpec((d, D_FF), const),
            pl.BlockSpec((D_FF, d), const),
            pl.BlockSpec((1, d), const),
        ],
        out_specs=pl.BlockSpec((tm, d), lambda i: (i, 0)),
        out_shape=jax.ShapeDtypeStruct((t, d), F32),
        compiler_params=_cparams(("arbitrary",)),
        name="mlp",
    )(x2, g, sc, sh, g2, w1, w2, fg)


def _rope_tables(seq):
    half = DIFF_HEAD_DIM // 2
    inv = 1.0 / (ROPE_THETA ** (jnp.arange(0, DIFF_HEAD_DIM, 2, dtype=F32) / DIFF_HEAD_DIM))
    ang = jnp.arange(seq, dtype=F32)[:, None] * inv[None, :]
    cos, sin = jnp.cos(ang), jnp.sin(ang)
    reps = LANES // half
    cos_t = jnp.tile(cos, (1, reps))
    sin_t = jnp.tile(jnp.concatenate([-sin, sin], axis=1), (1, reps // 2))
    return cos_t, sin_t


def _lambda_init(layer_idx):
    return 0.8 - 0.6 * math.exp(-0.3 * layer_idx)


def kernel(x, c, w_ada, b_ada, norm1_g, w_in, conv_w, conv_b, dt_bias, a_log, d_skip,
           ssm_norm_g, lam_q1, lam_k1, lam_q2, lam_k2, subln_g, w_ssm_out, w_attn_out,
           w_out, norm2_g, w_ff1, w_ff2, final_g):
    bsz, seq, d = x.shape
    depth = w_in.shape[0]
    assert d == D_MODEL and seq % 1024 == 0
    t = bsz * seq
    cos_t, sin_t = _rope_tables(seq)
    ada = _ada(c, w_ada, b_ada)
    x2 = x.reshape(t, d)
    pad_h = LANES - SSM_HEADS
    for l in range(depth):
        mods = [ada[l, :, i * d:(i + 1) * d].reshape(bsz, 1, d) for i in range(N_ADA)]
        sh1, sc1, g1, sh2, sc2, g2 = mods
        w_l = w_in[l]
        w_main = jnp.concatenate([w_l[:, :DT_OFF], w_l[:, DT_OFF + DT_W:]], axis=1).astype(BF16)
        w_dt = jnp.pad(w_l[:, DT_OFF:DT_OFF + DT_W], ((0, 0), (0, pad_h))).astype(BF16)
        proj, dt, qt, vt = _inproj(x2, norm1_g[l].reshape(1, d), sc1, sh1, w_main, w_dt,
                           cos_t, sin_t, conv_w[l], conv_b[l].reshape(1, -1), seq)
        yssm = _ssd(
            proj, dt,
            jnp.pad(dt_bias[l], (0, pad_h)).reshape(1, LANES),
            jnp.pad(a_log[l], (0, pad_h)).reshape(1, LANES),
            jnp.repeat(d_skip[l], SSM_HEAD_DIM).reshape(1, SSM_D_INNER),
            ssm_norm_g[l].reshape(1, SSM_D_INNER), bsz, seq)
        lam4 = jnp.stack([lam_q1[l], lam_k1[l], lam_q2[l], lam_k2[l]], axis=0)
        yatt = _attn_head(proj, qt, vt, lam4, subln_g[l].reshape(1, LANES), bsz, seq, _lambda_init(l))
        x2 = _merge(x2, yssm, yatt, proj, g1, w_ssm_out[l].astype(BF16),
                    w_attn_out[l].astype(BF16), w_out[l].astype(BF16), seq)
        x2 = _mlp(x2, norm2_g[l].reshape(1, d), sc2, sh2, g2, w_ff1[l].astype(BF16),
                  w_ff2[l].astype(BF16), final_g.reshape(1, d), seq, l == depth - 1)
    return x2.reshape(bsz, seq, d)
```

```python
import functools
import math

import jax
import jax.numpy as jnp
from jax import lax
from jax.experimental import pallas as pl
from jax.experimental.pallas import tpu as pltpu

F32 = jnp.float32
BF16 = jnp.bfloat16
SCORE_DTYPE = jnp.float8_e4m3fn
HIGHEST = lax.Precision.HIGHEST

D_MODEL = 1024
SSM_D_INNER = 2048
SSM_HEAD_DIM = 64
SSM_HEADS = 32
SSM_GROUPS = 4
SSM_STATE = 128
SSM_CONV = 4
SSM_CHUNK = 128
DIFF_HEADS = 8
DIFF_HEAD_DIM = 64
D_FF = 4096
N_ADA = 6
EPS = 1e-5
ROPE_THETA = 10000.0
DT_OFF = 5120
DT_W = SSM_HEADS
LOG2E = 1.4426950408889634

LANES = 128
HALO = 8
VMEM_LIMIT = 56 * 1024 * 1024

COL_Z, COL_XS, COL_BC, COL_Q, COL_K, COL_V, COL_GS, COL_GA = (
    0, 2048, 4096, 5120, 6144, 7168, 8192, 9216)
PROJ_W = 10240
OUT_K, OUT_GS, OUT_GA = 5120, 6144, 7168
OUT_W = 8192


def _out_col(col):
    assert not (COL_Q <= col < COL_K or COL_V <= col < COL_GS)
    if col < COL_Q:
        return col
    return col - (COL_K - OUT_K) if col < COL_V else col - (COL_GS - OUT_GS)


def _silu(x):
    h = 0.5 * x
    return h + h * jnp.tanh(h)


def _cparams(sem):
    return pltpu.CompilerParams(dimension_semantics=sem, vmem_limit_bytes=VMEM_LIMIT)


def _ada_kernel(c_ref, w_ref, b_ref, o_ref):
    c = c_ref[...]
    ca = c * jax.nn.sigmoid(c)
    o_ref[0] = jnp.dot(ca, w_ref[0], precision=HIGHEST,
                       preferred_element_type=F32) + b_ref[0]


def _ada(c, w_ada, b_ada):
    depth, d, n = w_ada.shape
    bsz = c.shape[0]
    tn = 1024
    return pl.pallas_call(
        _ada_kernel,
        grid=(depth, n // tn),
        in_specs=[
            pl.BlockSpec((bsz, d), lambda l, j: (0, 0)),
            pl.BlockSpec((1, d, tn), lambda l, j: (l, 0, j)),
            pl.BlockSpec((1, 1, tn), lambda l, j: (l, 0, j)),
        ],
        out_specs=pl.BlockSpec((1, bsz, tn), lambda l, j: (l, 0, j)),
        out_shape=jax.ShapeDtypeStruct((depth, bsz, n), F32),
        compiler_params=_cparams(("arbitrary", "arbitrary")),
        name="ada",
    )(c, w_ada, b_ada.reshape(depth, 1, n))


def _rope_tile(acc, cos, sin, scale):
    half = DIFF_HEAD_DIM // 2
    lane = lax.broadcasted_iota(jnp.int32, cos.shape, 1)
    first_half = (lane % DIFF_HEAD_DIM) < half
    outs = []
    for hh in range(acc.shape[1] // LANES):
        t = acc[:, hh * LANES:(hh + 1) * LANES]
        sw = jnp.where(first_half, pltpu.roll(t, LANES - half, 1), pltpu.roll(t, half, 1))
        outs.append((t * cos + sw * sin) * scale)
    return jnp.concatenate(outs, axis=1)


def _conv_silu_tile(pad_scr, hist_scr, cw_ref, cb_ref, ccs):
    tm = pad_scr.shape[0] - HALO
    nh = SSM_CONV - 1
    pad_scr[HALO - nh:HALO, :] = hist_scr[HALO - nh:HALO, ccs]
    y = cb_ref[:, ccs] + cw_ref[nh:nh + 1, ccs] * pad_scr[HALO:HALO + tm, :]
    for k in range(nh):
        off = HALO - nh + k
        y = y + cw_ref[k:k + 1, ccs] * pad_scr[off:off + tm, :]
    hist_scr[HALO - nh:HALO, ccs] = pad_scr[HALO + tm - nh:HALO + tm, :]
    return _silu(y)


def _inproj_kernel(x_ref, g_ref, sc_ref, sh_ref, w_ref, wdt_ref, cos_ref, sin_ref,
                   cw_ref, cb_ref, o_ref, dt_ref, qt_ref, vt_ref, pad_scr, hist_scr,
                   *, tn, nps, qscale):
    @pl.when(pl.program_id(0) % nps == 0)
    def _():
        hist_scr[...] = jnp.zeros_like(hist_scr)

    xf = x_ref[...]
    ms = jnp.mean(xf * xf, axis=-1, keepdims=True)
    h = xf * lax.rsqrt(ms + EPS) * g_ref[...]
    h = h * (1.0 + sc_ref[0]) + sh_ref[0]
    hb = h.astype(BF16)
    dt_ref[...] = jnp.dot(hb, wdt_ref[...], preferred_element_type=F32)
    cols = list(range(0, PROJ_W, tn))
    heavy = [c for c in cols if COL_XS <= c < COL_GS]
    light = [c for c in cols if not COL_XS <= c < COL_GS]
    order = []
    while heavy or light:
        if heavy:
            order.append(heavy.pop(0))
        if light:
            order.append(light.pop(0))
    tm = x_ref.shape[0]
    pending = None
    n_staged = 0
    for col in order + [None]:
        staged = None
        if col is not None:
            cs = slice(col, col + tn)
            acc = jnp.dot(hb, w_ref[:, cs], preferred_element_type=F32)
            if COL_XS <= col < COL_GS:
                slot = n_staged % pad_scr.shape[0]
                n_staged += 1
                pad_scr[slot, HALO:HALO + tm, :] = acc
                staged = (slot, col)
            else:
                ocol = _out_col(col)
                o_ref[:, ocol:ocol + tn] = acc.astype(o_ref.dtype)
        if pending is not None:
            slot, pcol = pending
            if pcol < COL_Q:
                y = _conv_silu_tile(pad_scr.at[slot], hist_scr, cw_ref, cb_ref,
                                    slice(pcol - COL_XS, pcol - COL_XS + tn))
            elif pcol < COL_V:
                y = _rope_tile(pad_scr[slot, HALO:HALO + tm, :], cos_ref[...], sin_ref[...],
                               qscale if pcol < COL_K else 1.0)
            else:
                y = pad_scr[slot, HALO:HALO + tm, :]
            if COL_Q <= pcol < COL_K:
                qt_ref[0, 0, pcol - COL_Q:pcol - COL_Q + tn, :] = y.astype(BF16).T
            elif COL_V <= pcol:
                vt_ref[0, 0, pcol - COL_V:pcol - COL_V + tn, :] = y.astype(BF16).T
            else:
                ocol = _out_col(pcol)
                o_ref[:, ocol:ocol + tn] = y.astype(o_ref.dtype)
        pending = staged


def _inproj(x2, g, sc, sh, w_main, w_dt, cos_t, sin_t, conv_w, conv_b, seq):
    t, d = x2.shape
    tm, tn = 512, 256
    nps = seq // tm
    conv_dim = conv_w.shape[1]
    qscale = (DIFF_HEAD_DIM ** -0.5) * LOG2E
    kern = functools.partial(_inproj_kernel, tn=tn, nps=nps, qscale=qscale)
    resident = dict(pipeline_mode=pl.Buffered(1))
    return pl.pallas_call(
        kern,
        grid=(t // tm,),
        in_specs=[
            pl.BlockSpec((tm, d), lambda i: (i, 0)),
            pl.BlockSpec((1, d), lambda i: (0, 0)),
            pl.BlockSpec((1, 1, d), lambda i: (i // nps, 0, 0)),
            pl.BlockSpec((1, 1, d), lambda i: (i // nps, 0, 0)),
            pl.BlockSpec((d, PROJ_W), lambda i: (0, 0), **resident),
            pl.BlockSpec((d, LANES), lambda i: (0, 0), **resident),
            pl.BlockSpec((tm, LANES), lambda i: (i % nps, 0)),
            pl.BlockSpec((tm, LANES), lambda i: (i % nps, 0)),
            pl.BlockSpec((SSM_CONV, conv_dim), lambda i: (0, 0)),
            pl.BlockSpec((1, conv_dim), lambda i: (0, 0)),
        ],
        out_specs=[
            pl.BlockSpec((tm, OUT_W), lambda i: (i, 0)),
            pl.BlockSpec((tm, LANES), lambda i: (i, 0)),
            pl.BlockSpec((1, 1, DIFF_HEADS * LANES, tm), lambda i: (i // nps, i % nps, 0, 0)),
            pl.BlockSpec((1, 1, DIFF_HEADS * LANES, tm), lambda i: (i // nps, i % nps, 0, 0)),
        ],
        out_shape=[
            jax.ShapeDtypeStruct((t, OUT_W), BF16),
            jax.ShapeDtypeStruct((t, LANES), F32),
            jax.ShapeDtypeStruct((t // seq, nps, DIFF_HEADS * LANES, tm), BF16),
            jax.ShapeDtypeStruct((t // seq, nps, DIFF_HEADS * LANES, tm), BF16),
        ],
        scratch_shapes=[
            pltpu.VMEM((4, HALO + tm, tn), F32),
            pltpu.VMEM((HALO, conv_dim), F32),
        ],
        compiler_params=_cparams(("arbitrary",)),
        name="inproj",
    )(x2, g, sc, sh, w_main, w_dt, cos_t, sin_t, conv_w, conv_b)


def _softplus(x):
    e = jnp.exp(-jnp.abs(x))
    u = 1.0 + e
    log1p_e = jnp.where(u == 1.0, e, jnp.log(u) * (e / (u - 1.0)))
    return jnp.maximum(x, 0.0) + log1p_e


def _split3(x):
    p0 = x.astype(BF16)
    r1 = x - p0.astype(F32)
    p1 = r1.astype(BF16)
    p2 = (r1 - p1.astype(F32)).astype(BF16)
    return p0, p1, p2


def _cumsum_rows(tri, x):
    return sum(jnp.dot(tri, p, preferred_element_type=F32) for p in _split3(x))


def _cumsum_cols(x, tri):
    return sum(jnp.dot(p, tri, preferred_element_type=F32) for p in _split3(x))


def _ssd_kernel(z_ref, xs_ref, bc_ref, dt_ref, dtb_ref, alog_ref, dexp_ref, ng_ref, o_ref,
                state_scr, y_scr):
    @pl.when(pl.program_id(1) == 0)
    def _():
        state_scr[...] = jnp.zeros_like(state_scr)

    for ci in range(z_ref.shape[0] // SSM_CHUNK):
        rows = pl.ds(ci * SSM_CHUNK, SSM_CHUNK)
        _ssd_chunk(z_ref.at[rows], xs_ref.at[rows], bc_ref.at[rows], dt_ref.at[rows],
                   dtb_ref, alog_ref, dexp_ref, ng_ref, o_ref.at[rows], state_scr,
                   y_scr.at[rows])


def _ssd_chunk(z_ref, xs_ref, bc_ref, dt_ref, dtb_ref, alog_ref, dexp_ref, ng_ref, o_ref,
               state_scr, y_scr):
    lc = SSM_CHUNK
    n = SSM_STATE
    xs_b = xs_ref[...]
    bc = bc_ref[...].astype(F32)

    dtp = _softplus(dt_ref[...] + dtb_ref[...])
    a_step = dtp * (-LOG2E * jnp.exp(alog_ref[...]))
    row_i = lax.broadcasted_iota(jnp.int32, (lc, lc), 0)
    col_i = lax.broadcasted_iota(jnp.int32, (lc, lc), 1)
    tril = row_i >= col_i
    tril_b = jnp.where(tril, 1.0, 0.0).astype(BF16)
    triu_b = jnp.where(row_i <= col_i, 1.0, 0.0).astype(BF16)
    acum = _cumsum_rows(tril_b, a_step)
    nh8 = SSM_HEADS
    a_step_t = a_step.T[:nh8, :]
    dt_t = dtp.T[:nh8, :]
    acum_t = _cumsum_cols(a_step_t, triu_b)
    alast_t = acum_t[:, lc - 1:lc]
    w2_t = dt_t * jnp.exp2(alast_t - acum_t)
    dlast = jnp.broadcast_to(jnp.exp2(alast_t), (nh8, LANES))
    arow_t = acum_t - jnp.log2(dt_t)

    lane = lax.broadcasted_iota(jnp.int32, (1, LANES), 1)
    lo = lane < SSM_HEAD_DIM
    heads_per_group = SSM_HEADS // SSM_GROUPS
    for g in range(SSM_GROUPS):
        bm = bc[:, g * n:(g + 1) * n]
        cm = bc[:, SSM_GROUPS * n + g * n:SSM_GROUPS * n + (g + 1) * n]
        bm_t = bm.T
        cb = jnp.dot(cm.astype(BF16), bm_t.astype(BF16), preferred_element_type=F32)
        for pp in range(heads_per_group // 2):
            p = g * (heads_per_group // 2) + pp
            cs = slice(p * LANES, (p + 1) * LANES)
            x_pair = xs_b[:, cs]
            s_pair = state_scr[:, cs]
            s_pair_b = s_pair.astype(BF16)
            y = jnp.zeros((lc, LANES), F32)
            s_new = jnp.zeros((n, LANES), F32)
            for e in range(2):
                h = 2 * p + e
                sel = lo if e == 0 else jnp.logical_not(lo)
                x_h = jnp.where(sel, x_pair, jnp.zeros_like(x_pair))
                s_h = jnp.where(sel, s_pair_b, jnp.zeros_like(s_pair_b))
                acol = jnp.broadcast_to(acum[:, h:h + 1], (lc, lc))
                seg = acol - arow_t[h:h + 1, :]
                decay_dt = jnp.exp2(jnp.where(tril, seg, -1e30))
                m_h = (decay_dt * cb).astype(BF16)
                y = y + jnp.dot(m_h, x_h, preferred_element_type=F32)
                c_h = (cm * jnp.exp2(acol)).astype(BF16)
                y = y + jnp.dot(c_h, s_h, preferred_element_type=F32)
                b_h = (bm_t * w2_t[h:h + 1, :]).astype(BF16)
                s_new = s_new + jnp.dot(b_h, x_h, preferred_element_type=F32)
            d_row = jnp.where(lo, dlast[2 * p:2 * p + 1, :], dlast[2 * p + 1:2 * p + 2, :])
            state_scr[:, cs] = s_pair * d_row + s_new
            y_scr[:, cs] = y + dexp_ref[:, cs] * x_pair.astype(F32)

    zf = z_ref[...].astype(F32)
    yg = y_scr[...] * _silu(zf)
    gw = SSM_D_INNER // SSM_GROUPS
    outs = []
    for g in range(SSM_GROUPS):
        blk = yg[:, g * gw:(g + 1) * gw]
        ms = jnp.mean(blk * blk, axis=-1, keepdims=True)
        outs.append(blk * lax.rsqrt(ms + EPS))
    o_ref[...] = (jnp.concatenate(outs, axis=1) * ng_ref[...]).astype(o_ref.dtype)


def _ssd(proj, dt, dtb, alog, dexp, ng, bsz, seq):
    t = proj.shape[0]
    lc = 8 * SSM_CHUNK
    nc = seq // lc
    xw, bw = SSM_D_INNER, 2 * SSM_GROUPS * SSM_STATE
    row = lambda b, c: b * nc + c
    full = lambda b, c: (0, 0)
    return pl.pallas_call(
        _ssd_kernel,
        grid=(bsz, nc),
        in_specs=[
            pl.BlockSpec((lc, xw), lambda b, c: (row(b, c), COL_Z // xw)),
            pl.BlockSpec((lc, xw), lambda b, c: (row(b, c), COL_XS // xw)),
            pl.BlockSpec((lc, bw), lambda b, c: (row(b, c), COL_BC // bw)),
            pl.BlockSpec((lc, LANES), lambda b, c: (row(b, c), 0)),
            pl.BlockSpec((1, LANES), full),
            pl.BlockSpec((1, LANES), full),
            pl.BlockSpec((1, xw), full),
            pl.BlockSpec((1, xw), full),
        ],
        out_specs=pl.BlockSpec((lc, xw), lambda b, c: (row(b, c), 0)),
        out_shape=jax.ShapeDtypeStruct((t, xw), BF16),
        scratch_shapes=[
            pltpu.VMEM((SSM_STATE, xw), F32),
            pltpu.VMEM((lc, xw), F32),
        ],
        compiler_params=_cparams(("arbitrary", "arbitrary")),
        name="ssd",
    )(proj, proj, proj, dt, dtb, alog, dexp, ng)


def _attn_head_kernel(qt_ref, k_ref, vt_ref, lam_ref, sg_ref, o_ref,
                      qt_scr, kf_scr, m_scr, l_scr, acc_scr, sa_scr, sb_scr,
                      *, tk, nsub, cw, lam_init):
    tq = nsub * tk
    w = 2 * tq
    assert nsub % 2 == 0 and tk % cw == 0
    nq = qt_scr.shape[0]

    def prep_q(u, carry):
        qt = jnp.concatenate([qt_ref[0, nsub * u + i] for i in range(nsub)], axis=1)
        comp = lax.broadcasted_iota(jnp.int32, qt.shape, 0) < DIFF_HEAD_DIM
        zero = jnp.zeros_like(qt)
        qt_scr[u] = jnp.concatenate([jnp.where(comp, qt, zero), jnp.where(comp, zero, qt)],
                                    axis=1).astype(qt_scr.dtype)
        return carry
    lax.fori_loop(0, nq, prep_q, 0)

    def prep_k(j, carry):
        start = pl.multiple_of(j * tk, tk)
        kf_scr[j] = k_ref[pl.ds(start, tk), :].astype(kf_scr.dtype)
        return carry
    lax.fori_loop(0, kf_scr.shape[0], prep_k, 0)

    def reset_state():
        m_scr[...] = jnp.full(m_scr.shape, -1e30, F32)
        l_scr[...] = jnp.zeros(l_scr.shape, F32)
        acc_scr[...] = jnp.zeros(acc_scr.shape, F32)

    def k_block(j):
        return kf_scr[j]

    def scores(u, j, s_ref):
        s_ref[...] = jnp.dot(k_block(j), qt_scr[u], preferred_element_type=F32)

    def chunk_row(c):
        return (c * cw) % tq

    def score_chunks(u, j, s_ref, min_row):
        def one(c):
            cs = slice(c * cw, (c + 1) * cw)
            def run():
                s_ref[:, cs] = jnp.dot(k_block(j), qt_scr[u, :, cs], preferred_element_type=F32)
            return run
        return [one(c) for c in range(w // cw) if chunk_row(c) >= min_row]

    def interleave(first, second):
        for i in range(max(len(first), len(second))):
            if i < len(first):
                first[i]()
            if i < len(second):
                second[i]()

    def update_chunk(s_ref, j, c, nkeys, row_off):
        cs = slice(c * cw, (c + 1) * cw)
        s = s_ref[0:nkeys, cs]
        if row_off is not None:
            key = lax.broadcasted_iota(jnp.int32, (nkeys, cw), 0)
            row = lax.broadcasted_iota(jnp.int32, (nkeys, cw), 1) + row_off
            s = jnp.where(key <= row, s, -1e30)
        m_old = m_scr[:, cs]
        m_new = jnp.maximum(m_old, jnp.max(s, axis=0, keepdims=True))
        alpha = jnp.exp2(m_old - m_new)
        p = jnp.exp2(s - m_new)
        l_scr[:, cs] = alpha * l_scr[:, cs] + jnp.sum(p, axis=0, keepdims=True)
        acc_scr[:, cs] = alpha * acc_scr[:, cs] + jnp.dot(
            vt_ref[0, j, :, 0:nkeys], p.astype(BF16), preferred_element_type=F32)
        m_scr[:, cs] = m_new

    def update_full(j, s_ref):
        return [functools.partial(update_chunk, s_ref, j, c, tk, None) for c in range(w // cw)]

    def update_diag(j, s_ref, i):
        out = []
        for c in range(w // cw):
            row_off = chunk_row(c) - i * tk
            if 0 <= row_off < tk:
                out.append(functools.partial(update_chunk, s_ref, j, c, row_off + cw, row_off))
            elif row_off >= tk:
                out.append(functools.partial(update_chunk, s_ref, j, c, tk, None))
        return out

    def finalize(u):
        lam = (jnp.exp(jnp.sum(lam_ref[0:1, :] * lam_ref[1:2, :], axis=-1, keepdims=True))
               - jnp.exp(jnp.sum(lam_ref[2:3, :] * lam_ref[3:4, :], axis=-1, keepdims=True))
               + lam_init)
        ot = acc_scr[...] / l_scr[...]
        o = (ot[:, :tq] - lam * ot[:, tq:]).T
        ms = jnp.mean(o * o, axis=-1, keepdims=True)
        o = o * lax.rsqrt(ms + EPS) * sg_ref[...] * (1.0 - lam_init)
        o_ref[pl.ds(pl.multiple_of(u * tq, tq), tq), :] = o.astype(o_ref.dtype)

    reset_state()
    scores(0, 0, sa_scr)

    def tile(u, carry):
        def pair(j):
            interleave(score_chunks(u, j + 1, sb_scr, 0), update_full(j, sa_scr))
            interleave(score_chunks(u, j + 2, sa_scr, 0), update_full(j + 1, sb_scr))

        npairs = (nsub // 2) * u

        def two_pairs(t2, c2):
            pair(4 * t2)
            pair(4 * t2 + 2)
            return c2
        lax.fori_loop(0, npairs // 2, two_pairs, 0)

        @pl.when(npairs % 2 == 1)
        def _():
            pair(2 * (npairs - 1))

        jd = nsub * u
        u_next = jnp.minimum(u + 1, nq - 1)
        for i in range(nsub):
            cur, nxt = (sa_scr, sb_scr) if i % 2 == 0 else (sb_scr, sa_scr)
            if i + 1 < nsub:
                ahead = score_chunks(u, jd + i + 1, nxt, (i + 1) * tk)
            else:
                ahead = score_chunks(u_next, 0, nxt, 0)
            interleave(ahead, update_diag(jd + i, cur, i))
        finalize(u)
        reset_state()
        return carry

    lax.fori_loop(0, nq, tile, 0)


def _attn_head(proj, qt, vt, lam4, sg, bsz, seq, lam_init):
    t = proj.shape[0]
    nkv, tk = qt.shape[1], qt.shape[3]
    nsub = 2
    tq = nsub * tk
    cw = 256
    kern = functools.partial(_attn_head_kernel, tk=tk, nsub=nsub, cw=cw, lam_init=lam_init)
    return pl.pallas_call(
        kern,
        grid=(bsz, DIFF_HEADS),
        in_specs=[
            pl.BlockSpec((1, nkv, LANES, tk), lambda b, h: (b, 0, h, 0)),
            pl.BlockSpec((seq, LANES), lambda b, h: (b, OUT_K // LANES + h)),
            pl.BlockSpec((1, nkv, LANES, tk), lambda b, h: (b, 0, h, 0)),
            pl.BlockSpec((4, DIFF_HEAD_DIM), lambda b, h: (0, 0)),
            pl.BlockSpec((1, LANES), lambda b, h: (0, 0)),
        ],
        out_specs=pl.BlockSpec((seq, LANES), lambda b, h: (b, h)),
        out_shape=jax.ShapeDtypeStruct((t, DIFF_HEADS * LANES), BF16),
        scratch_shapes=[
            pltpu.VMEM((seq // tq, LANES, 2 * tq), SCORE_DTYPE),
            pltpu.VMEM((nkv, tk, LANES), SCORE_DTYPE),
            pltpu.VMEM((1, 2 * tq), F32),
            pltpu.VMEM((1, 2 * tq), F32),
            pltpu.VMEM((LANES, 2 * tq), F32),
            pltpu.VMEM((tk, 2 * tq), F32),
            pltpu.VMEM((tk, 2 * tq), F32),
        ],
        compiler_params=_cparams(("arbitrary", "arbitrary")),
        name="attn",
    )(qt, proj, vt, lam4, sg)


def _merge_kernel(x_ref, ys_ref, ya_ref, gs_ref, ga_ref, g1_ref, ws_ref, wa_ref, wo_ref, o_ref):
    ys = jnp.dot(ys_ref[...], ws_ref[...], preferred_element_type=F32)
    ya = jnp.dot(ya_ref[...], wa_ref[...], preferred_element_type=F32)
    mixed = (jax.nn.sigmoid(gs_ref[...].astype(F32)) * ys
             + jax.nn.sigmoid(ga_ref[...].astype(F32)) * ya)
    upd = jnp.dot(mixed.astype(BF16), wo_ref[...], preferred_element_type=F32)
    o_ref[...] = x_ref[...] + g1_ref[0] * upd


def _merge(x2, yssm, yatt, proj, g1, ws, wa, wo, seq):
    t, d = x2.shape
    tm = 512
    nps = seq // tm
    const = lambda i: (0, 0)
    return pl.pallas_call(
        _merge_kernel,
        grid=(t // tm,),
        in_specs=[
            pl.BlockSpec((tm, d), lambda i: (i, 0)),
            pl.BlockSpec((tm, SSM_D_INNER), lambda i: (i, 0)),
            pl.BlockSpec((tm, d), lambda i: (i, 0)),
            pl.BlockSpec((tm, d), lambda i: (i, OUT_GS // d)),
            pl.BlockSpec((tm, d), lambda i: (i, OUT_GA // d)),
            pl.BlockSpec((1, 1, d), lambda i: (i // nps, 0, 0)),
            pl.BlockSpec((SSM_D_INNER, d), const),
            pl.BlockSpec((d, d), const),
            pl.BlockSpec((d, d), const),
        ],
        out_specs=pl.BlockSpec((tm, d), lambda i: (i, 0)),
        out_shape=jax.ShapeDtypeStruct((t, d), F32),
        compiler_params=_cparams(("arbitrary",)),
        name="merge",
    )(x2, yssm, yatt, proj, proj, g1, ws, wa, wo)


def _mlp_kernel(x_ref, g_ref, sc_ref, sh_ref, g2_ref, w1_ref, w2_ref, fg_ref, o_ref, *, final):
    xf = x_ref[...]
    ms = jnp.mean(xf * xf, axis=-1, keepdims=True)
    h = xf * lax.rsqrt(ms + EPS) * g_ref[...]
    h = h * (1.0 + sc_ref[0]) + sh_ref[0]
    u = jnp.dot(h.astype(BF16), w1_ref[...], preferred_element_type=F32)
    u = jnp.square(jnp.maximum(u, 0.0))
    y = xf + g2_ref[0] * jnp.dot(u.astype(BF16), w2_ref[...], preferred_element_type=F32)
    if final:
        ms2 = jnp.mean(y * y, axis=-1, keepdims=True)
        y = y * lax.rsqrt(ms2 + EPS) * fg_ref[...]
    o_ref[...] = y


def _mlp(x2, g, sc, sh, g2, w1, w2, fg, seq, final):
    t, d = x2.shape
    tm = 512
    nps = seq // tm
    const = lambda i: (0, 0)
    per_b = lambda i: (i // nps, 0, 0)
    return pl.pallas_call(
        functools.partial(_mlp_kernel, final=final),
        grid=(t // tm,),
        in_specs=[
            pl.BlockSpec((tm, d), lambda i: (i, 0)),
            pl.BlockSpec((1, d), const),
            pl.BlockSpec((1, 1, d), per_b),
            pl.BlockSpec((1, 1, d), per_b),
            pl.BlockSpec((1, 1, d), per_b),
            pl.BlockSpec((d, D_FF), const),
            pl.BlockSpec((D_FF, d), const),
            pl.BlockSpec((1, d), const),
        ],
        out_specs=pl.BlockSpec((tm, d), lambda i: (i, 0)),
        out_shape=jax.ShapeDtypeStruct((t, d), F32),
        compiler_params=_cparams(("arbitrary",)),
        name="mlp",
    )(x2, g, sc, sh, g2, w1, w2, fg)


def _rope_tables(seq):
    half = DIFF_HEAD_DIM // 2
    inv = 1.0 / (ROPE_THETA ** (jnp.arange(0, DIFF_HEAD_DIM, 2, dtype=F32) / DIFF_HEAD_DIM))
    ang = jnp.arange(seq, dtype=F32)[:, None] * inv[None, :]
    cos, sin = jnp.cos(ang), jnp.sin(ang)
    reps = LANES // half
    cos_t = jnp.tile(cos, (1, reps))
    sin_t = jnp.tile(jnp.concatenate([-sin, sin], axis=1), (1, reps // 2))
    return cos_t, sin_t


def _lambda_init(layer_idx):
    return 0.8 - 0.6 * math.exp(-0.3 * layer_idx)


def kernel(x, c, w_ada, b_ada, norm1_g, w_in, conv_w, conv_b, dt_bias, a_log, d_skip,
           ssm_norm_g, lam_q1, lam_k1, lam_q2, lam_k2, subln_g, w_ssm_out, w_attn_out,
           w_out, norm2_g, w_ff1, w_ff2, final_g):
    bsz, seq, d = x.shape
    depth = w_in.shape[0]
    assert d == D_MODEL and seq % 1024 == 0
    t = bsz * seq
    cos_t, sin_t = _rope_tables(seq)
    ada = _ada(c, w_ada, b_ada)
    x2 = x.reshape(t, d)
    pad_h = LANES - SSM_HEADS
    for l in range(depth):
        mods = [ada[l, :, i * d:(i + 1) * d].reshape(bsz, 1, d) for i in range(N_ADA)]
        sh1, sc1, g1, sh2, sc2, g2 = mods
        w_l = w_in[l]
        w_main = jnp.concatenate([w_l[:, :DT_OFF], w_l[:, DT_OFF + DT_W:]], axis=1).astype(BF16)
        w_dt = jnp.pad(w_l[:, DT_OFF:DT_OFF + DT_W], ((0, 0), (0, pad_h))).astype(BF16)
        proj, dt, qt, vt = _inproj(x2, norm1_g[l].reshape(1, d), sc1, sh1, w_main, w_dt,
                           cos_t, sin_t, conv_w[l], conv_b[l].reshape(1, -1), seq)
        yssm = _ssd(
            proj, dt,
            jnp.pad(dt_bias[l], (0, pad_h)).reshape(1, LANES),
            jnp.pad(a_log[l], (0, pad_h)).reshape(1, LANES),
            jnp.repeat(d_skip[l], SSM_HEAD_DIM).reshape(1, SSM_D_INNER),
            ssm_norm_g[l].reshape(1, SSM_D_INNER), bsz, seq)
        lam4 = jnp.stack([lam_q1[l], lam_k1[l], lam_q2[l], lam_k2[l]], axis=0)
        yatt = _attn_head(proj, qt, vt, lam4, subln_g[l].reshape(1, LANES), bsz, seq, _lambda_init(l))
        x2 = _merge(x2, yssm, yatt, proj, g1, w_ssm_out[l].astype(BF16),
                    w_attn_out[l].astype(BF16), w_out[l].astype(BF16), seq)
        x2 = _mlp(x2, norm2_g[l].reshape(1, d), sc2, sh2, g2, w_ff1[l].astype(BF16),
                  w_ff2[l].astype(BF16), final_g.reshape(1, d), seq, l == depth - 1)
    return x2.reshape(bsz, seq, d)
```
